```python
import functools
import jax, jax.numpy as jnp
from jax import lax
import numpy as np

D_MODEL = 1024
BATCH = 8
SEQ = 2048
DEPTH = 4
DEC_BATCH = 32
DEC_SEQ = 1
PAST_LEN = 8192
PAGE_SIZE = 128

CONV_W = D_MODEL // 4
CONV_K = 31
POOL_W = D_MODEL // 4
POOL_WINDOWS = (2, 4, 8, 16)
POOL_GROUPS = len(POOL_WINDOWS)
POOL_GW = POOL_W // POOL_GROUPS
POOL_BUF = max(POOL_WINDOWS) - 1
HEAD_DIM = 64
ATTN_W = D_MODEL - CONV_W - POOL_W
N_HEADS = ATTN_W // HEAD_DIM
N_KV_HEADS = 2
IDX_HEADS = 4
IDX_DIM = 64
INDEX_TOPK = 256
ROPE_THETA = 500000.0
ROPE_FRAC = 4
Q_BLOCK = 128
D_FF = 2816
D_MIX = CONV_W + POOL_W + ATTN_W
NORM_EPS = 1e-6
IN_SPLITS = (2 * CONV_W, POOL_W, ATTN_W, N_KV_HEADS * HEAD_DIM, N_KV_HEADS * HEAD_DIM,
             IDX_HEADS * IDX_DIM, IDX_DIM, IDX_HEADS)
D_IN = sum(IN_SPLITS)

kernel_name = 'hymba_conv_pool_dsa_decoder_step'


def rmsnorm(x, g):
    xf = x.astype(jnp.float32)
    y = xf * lax.rsqrt(jnp.mean(xf * xf, axis=-1, keepdims=True) + NORM_EPS)
    return (y * g.astype(jnp.float32)).astype(x.dtype)


def layernorm(x, g, b):
    xf = x.astype(jnp.float32)
    mu = jnp.mean(xf, axis=-1, keepdims=True)
    var = jnp.mean(jnp.square(xf - mu), axis=-1, keepdims=True)
    y = (xf - mu) * lax.rsqrt(var + NORM_EPS)
    return (y * g.astype(jnp.float32) + b.astype(jnp.float32)).astype(x.dtype)


def swiglu(x, wg, wu, wd):
    return (jax.nn.silu(x @ wg) * (x @ wu)) @ wd


def partial_rope(x, pos):
    rot = x.shape[-1] // ROPE_FRAC
    half = rot // 2
    inv = ROPE_THETA ** (-jnp.arange(half, dtype=jnp.float32) * 2.0 / rot)
    ang = pos.astype(jnp.float32)[:, None] * inv[None, :]
    cos = jnp.cos(ang)[:, None, :]
    sin = jnp.sin(ang)[:, None, :]
    xf = x.astype(jnp.float32)
    x1, x2 = xf[..., :half], xf[..., half:rot]
    out = jnp.concatenate([x1 * cos - x2 * sin, x2 * cos + x1 * sin, xf[..., rot:]], axis=-1)
    return out.astype(x.dtype)


def conv_mixer(a, prev, dw, dw_bias, ln_g, ln_b, pw):
    g = a[..., :CONV_W] * jax.nn.sigmoid(a[..., CONV_W:])
    buf = jnp.concatenate([prev.astype(g.dtype), g], axis=1)
    y = lax.conv_general_dilated(buf, dw[:, None, :].astype(g.dtype), window_strides=(1,),
                                 padding='VALID', dimension_numbers=('NWC', 'WIO', 'NWC'),
                                 feature_group_count=CONV_W) + dw_bias
    y = jax.nn.silu(layernorm(y, ln_g, ln_b)) @ pw
    return y, buf[:, -(CONV_K - 1):]


def pool_mixer(xb, prev, pos, pool_w, pool_scale):
    B, T, _ = xb.shape
    buf = jnp.concatenate([prev.astype(xb.dtype), xb], axis=1)
    xf = buf.astype(jnp.float32)
    cs = jnp.concatenate([jnp.zeros((B, 1, POOL_W), jnp.float32), jnp.cumsum(xf, axis=1)], axis=1)
    hi = POOL_BUF + 1 + jnp.arange(T)
    means = []
    for g, w in enumerate(POOL_WINDOWS):
        c = slice(g * POOL_GW, (g + 1) * POOL_GW)
        cnt = jnp.minimum(w, pos + 1).astype(jnp.float32)
        means.append((cs[:, hi, c] - cs[:, hi - w, c]) / cnt[None, :, None])
    pooled = (jnp.concatenate(means, axis=-1) - xf[:, POOL_BUF:]).astype(xb.dtype)
    y = jnp.einsum('btgc,gcd->btgd', pooled.reshape(B, T, POOL_GROUPS, POOL_GW), pool_w)
    y = y.reshape(B, T, POOL_W) * pool_scale
    return y, buf[:, -POOL_BUF:]


def indexer_scores(qi, wi, ki):
    d = jnp.einsum('bthd,bsd->bths', qi.astype(jnp.float32), ki.astype(jnp.float32)) * IDX_DIM ** -0.5
    return jnp.einsum('bths,bth->bts', jax.nn.relu(d), wi.astype(jnp.float32) * IDX_HEADS ** -0.5)


def sparse_attend(q, ks, vs, valid):
    B, T = q.shape[:2]
    qg = q.reshape(B, T, N_KV_HEADS, N_HEADS // N_KV_HEADS, HEAD_DIM).astype(jnp.float32)
    s = jnp.einsum('btngd,btknd->btngk', qg, ks.astype(jnp.float32)) * HEAD_DIM ** -0.5
    s = jnp.where(valid[:, :, None, None, :], s, -jnp.inf)
    p = jax.nn.softmax(s, axis=-1)
    o = jnp.einsum('btngk,btknd->btngd', p, vs.astype(jnp.float32))
    return o.reshape(B, T, ATTN_W).astype(q.dtype)


def gather_rows(rows, idx):
    return jax.vmap(lambda r, i: r[i])(rows, idx)


def dsa_prompt(q, k, v, qi, ki, wi, pos):
    B, T = q.shape[:2]
    topk = min(INDEX_TOPK, T // 4)
    nb = T // Q_BLOCK

    def to_blocks(t):
        return t.reshape((B, nb, Q_BLOCK) + t.shape[2:]).swapaxes(0, 1)

    def block(args):
        qb, qib, wb, pb = args
        sc = indexer_scores(qib, wb, ki)
        sc = jnp.where(pos[None, None, :] <= pb[None, :, None], sc, -jnp.inf)
        _, sel = lax.top_k(sc, topk)
        valid = sel <= pb[None, :, None]
        return sparse_attend(qb, gather_rows(k, sel), gather_rows(v, sel), valid)

    out = lax.map(block, (to_blocks(q), to_blocks(qi), to_blocks(wi), pos.reshape(nb, Q_BLOCK)))
    return out.swapaxes(0, 1).reshape(B, T, ATTN_W)


def dsa_sample(q, k, v, qi, ki, wi, pos, cache_k, cache_v, cache_kidx, page_table):
    B, T = q.shape[:2]
    past = page_table.shape[1] * PAGE_SIZE
    topk = min(INDEX_TOPK, (past + T) // 4)
    ki_past = cache_kidx[page_table].reshape(B, past, IDX_DIM)
    ki_all = jnp.concatenate([ki_past.astype(ki.dtype), ki], axis=1)
    sc = indexer_scores(qi, wi, ki_all)
    kpos = jnp.arange(past + T)
    sc = jnp.where(kpos[None, None, :] <= pos[None, :, None], sc, -jnp.inf)
    _, sel = lax.top_k(sc, topk)
    valid = sel <= pos[None, :, None]
    is_new = sel >= past
    old_idx = jnp.minimum(sel, past - 1)
    phys = jnp.take_along_axis(page_table, (old_idx // PAGE_SIZE).reshape(B, -1), axis=1).reshape(sel.shape)
    off = old_idx % PAGE_SIZE
    new_idx = jnp.clip(sel - past, 0, T - 1)

    def gather(cache, new):
        old = cache[phys, off].astype(new.dtype)
        return jnp.where(is_new[..., None, None], gather_rows(new, new_idx), old)

    return sparse_attend(q, gather(cache_k, k), gather(cache_v, v), valid)


def token_mixing(u, pos, conv_prev, pool_prev, w_in, conv_dw, conv_dw_bias, conv_ln_gain, conv_ln_bias,
                 conv_pw, pool_w, pool_scale, w_out, attend):
    B, T, _ = u.shape
    a = u @ w_in
    offs = [int(o) for o in np.cumsum(IN_SPLITS)[:-1]]
    a_conv, a_pool, a_q, a_k, a_v, a_qi, a_ki, a_wi = jnp.split(a, offs, axis=-1)
    q = partial_rope(a_q.reshape(B, T, N_HEADS, HEAD_DIM), pos)
    k = partial_rope(a_k.reshape(B, T, N_KV_HEADS, HEAD_DIM), pos)
    v = a_v.reshape(B, T, N_KV_HEADS, HEAD_DIM)
    qi = partial_rope(a_qi.reshape(B, T, IDX_HEADS, IDX_DIM), pos)
    ki = partial_rope(a_ki.reshape(B, T, 1, IDX_DIM), pos)[:, :, 0]
    y_c, conv_state = conv_mixer(a_conv, conv_prev, conv_dw, conv_dw_bias, conv_ln_gain, conv_ln_bias, conv_pw)
    y_p, pool_state = pool_mixer(a_pool, pool_prev, pos, pool_w, pool_scale)
    y_a = attend(q, k, v, qi, ki, a_wi, pos)
    y = jnp.concatenate([y_c, y_p, y_a], axis=-1) @ w_out
    return y, (k, v, ki, conv_state, pool_state)


def macaron_ffn(x, p):
    npre, wg, wu, wd, npost = p
    return x + 0.5 * rmsnorm(swiglu(rmsnorm(x, npre), wg, wu, wd), npost)


def layer(x, pos, conv_prev, pool_prev, lp, attend):
    h = macaron_ffn(x, lp['ffn1'])
    npre, mix_w, npost = lp['mix']
    y, state = token_mixing(rmsnorm(h, npre), pos, conv_prev, pool_prev, *mix_w, attend)
    h = h + rmsnorm(y, npost)
    h = macaron_ffn(h, lp['ffn2'])
    return h, state


def setup_inputs(seed: int = 0) -> dict:
    key = jax.random.key(seed)
    ks = iter(jax.random.split(key, 40))
    n_pages = PAST_LEN // PAGE_SIZE
    n_phys = (DEC_BATCH * n_pages * 5) // 4

    def nrm(shape, scale):
        return jax.random.normal(next(ks), shape, jnp.float32) * scale

    def gain(shape, s=0.02):
        return 1.0 + nrm(shape, s)

    d = {}
    d['x_prompt'] = nrm((BATCH, SEQ, D_MODEL), 1.0)
    d['x_sample'] = nrm((DEC_BATCH, DEC_SEQ, D_MODEL), 1.0)
    d['cache_k'] = nrm((DEPTH, n_phys, PAGE_SIZE, N_KV_HEADS, HEAD_DIM), 1.0)
    d['cache_v'] = nrm((DEPTH, n_phys, PAGE_SIZE, N_KV_HEADS, HEAD_DIM), 1.0)
    d['cache_kidx'] = nrm((DEPTH, n_phys, PAGE_SIZE, IDX_DIM), 1.0)
    d['state_conv'] = nrm((DEPTH, DEC_BATCH, CONV_K - 1, CONV_W), 1.0)
    d['state_pool'] = nrm((DEPTH, DEC_BATCH, POOL_BUF, POOL_W), 1.0)
    perm = jax.random.permutation(next(ks), n_phys)[: DEC_BATCH * n_pages]
    d['page_table'] = perm.reshape(DEC_BATCH, n_pages).astype(jnp.int32)
    d['ffn1_norm_pre'] = gain((DEPTH, D_MODEL))
    d['ffn1_w_gate'] = nrm((DEPTH, D_MODEL, D_FF), D_MODEL ** -0.5)
    d['ffn1_w_up'] = nrm((DEPTH, D_MODEL, D_FF), D_MODEL ** -0.5)
    d['ffn1_w_down'] = nrm((DEPTH, D_FF, D_MODEL), D_FF ** -0.5)
    d['ffn1_norm_post'] = gain((DEPTH, D_MODEL))
    d['mix_norm_pre'] = gain((DEPTH, D_MODEL))
    d['w_in'] = nrm((DEPTH, D_MODEL, D_IN), D_MODEL ** -0.5)
    d['conv_dw'] = nrm((DEPTH, CONV_K, CONV_W), CONV_K ** -0.5)
    d['conv_dw_bias'] = nrm((DEPTH, CONV_W), 0.01)
    d['conv_ln_gain'] = gain((DEPTH, CONV_W))
    d['conv_ln_bias'] = nrm((DEPTH, CONV_W), 0.01)
    d['conv_pw'] = nrm((DEPTH, CONV_W, CONV_W), CONV_W ** -0.5)
    d['pool_w'] = nrm((DEPTH, POOL_GROUPS, POOL_GW, POOL_GW), POOL_GW ** -0.5)
    d['pool_scale'] = gain((DEPTH, POOL_W), 0.1)
    d['w_out'] = nrm((DEPTH, D_MIX, D_MODEL), D_MIX ** -0.5)
    d['mix_norm_post'] = gain((DEPTH, D_MODEL))
    d['ffn2_norm_pre'] = gain((DEPTH, D_MODEL))
    d['ffn2_w_gate'] = nrm((DEPTH, D_MODEL, D_FF), D_MODEL ** -0.5)
    d['ffn2_w_up'] = nrm((DEPTH, D_MODEL, D_FF), D_MODEL ** -0.5)
    d['ffn2_w_down'] = nrm((DEPTH, D_FF, D_MODEL), D_FF ** -0.5)
    d['ffn2_norm_post'] = gain((DEPTH, D_MODEL))
    return d


def reference(x_prompt, x_sample, cache_k, cache_v, cache_kidx, state_conv, state_pool, page_table,
              ffn1_norm_pre, ffn1_w_gate, ffn1_w_up, ffn1_w_down, ffn1_norm_post,
              mix_norm_pre, w_in, conv_dw, conv_dw_bias, conv_ln_gain, conv_ln_bias, conv_pw,
              pool_w, pool_scale, w_out, mix_norm_post,
              ffn2_norm_pre, ffn2_w_gate, ffn2_w_up, ffn2_w_down, ffn2_norm_post):
    Bp, Tp = x_prompt.shape[:2]
    Bs, Ts = x_sample.shape[:2]
    past = page_table.shape[1] * PAGE_SIZE
    pos_p = jnp.arange(Tp)
    pos_s = past + jnp.arange(Ts)
    conv0 = jnp.zeros((Bp, CONV_K - 1, CONV_W), x_prompt.dtype)
    pool0 = jnp.zeros((Bp, POOL_BUF, POOL_W), x_prompt.dtype)
    hp, hs = x_prompt, x_sample
    sp = ([], [], [], [], [])
    ss = ([], [], [], [], [])
    for l in range(DEPTH):
        lp = {
            'ffn1': (ffn1_norm_pre[l], ffn1_w_gate[l], ffn1_w_up[l], ffn1_w_down[l], ffn1_norm_post[l]),
            'mix': (mix_norm_pre[l],
                    (w_in[l], conv_dw[l], conv_dw_bias[l], conv_ln_gain[l], conv_ln_bias[l], conv_pw[l],
                     pool_w[l], pool_scale[l], w_out[l]),
                    mix_norm_post[l]),
            'ffn2': (ffn2_norm_pre[l], ffn2_w_gate[l], ffn2_w_up[l], ffn2_w_down[l], ffn2_norm_post[l]),
        }
        attend_s = functools.partial(dsa_sample, cache_k=cache_k[l], cache_v=cache_v[l],
                                     cache_kidx=cache_kidx[l], page_table=page_table)
        hp, st_p = layer(hp, pos_p, conv0, pool0, lp, dsa_prompt)
        hs, st_s = layer(hs, pos_s, state_conv[l], state_pool[l], lp, attend_s)
        for acc, val in zip(sp, st_p):
            acc.append(val)
        for acc, val in zip(ss, st_s):
            acc.append(val)
    return (hp, hs,
            jnp.stack(sp[0]), jnp.stack(sp[1]), jnp.stack(sp[2]), jnp.stack(sp[3]), jnp.stack(sp[4]),
            jnp.stack(ss[0]), jnp.stack(ss[1]), jnp.stack(ss[2]), jnp.stack(ss[3]), jnp.stack(ss[4]))
```

```python
import functools

import jax
import jax.numpy as jnp
from jax import lax
from jax.experimental import pallas as pl
from jax.experimental.pallas import tpu as pltpu

D_MODEL = 1024
PAGE_SIZE = 128
CONV_W = D_MODEL // 4
CONV_K = 31
POOL_W = D_MODEL // 4
POOL_WINDOWS = (2, 4, 8, 16)
POOL_GW = POOL_W // len(POOL_WINDOWS)
POOL_BUF = max(POOL_WINDOWS) - 1
HEAD_DIM = 64
ATTN_W = D_MODEL - CONV_W - POOL_W
N_HEADS = ATTN_W // HEAD_DIM
N_KV_HEADS = 2
IDX_HEADS = 4
IDX_DIM = 64
INDEX_TOPK = 256
ROPE_THETA = 500000.0
ROPE_ROT = HEAD_DIM // 4
ROPE_HALF = ROPE_ROT // 2
Q_BLOCK = 128
D_FF = 2816
NORM_EPS = 1e-6

Q_SCALE = HEAD_DIM ** -0.5 * 1.4426950408889634

LANES = 128
V7X_VMEM_BYTES = 64 * 1024 * 1024
VMEM_LIMIT = V7X_VMEM_BYTES - 8 * 1024 * 1024
INT32_MIN = -(2 ** 31)

C_CONV_V, C_CONV_G, C_POOL, C_Q, C_K, C_V, C_QI, C_KIWI, C_END = (
    0, 2 * LANES, 4 * LANES, 6 * LANES, 10 * LANES, 11 * LANES, 12 * LANES, 16 * LANES, 17 * LANES)

F32 = jnp.float32
BF16 = jnp.bfloat16


def _layer_spec(layer, shape):
    return pl.BlockSpec((None,) + shape, lambda *_: (layer,) + (0,) * len(shape),
                        pipeline_mode=pl.Buffered(1))


def _rms(x, g):
    ms = jnp.mean(x * x, axis=-1, keepdims=True)
    return x * lax.rsqrt(ms + NORM_EPS) * g


def _swiglu(xn, wg_ref, wu_ref, wd_ref):
    g = jnp.dot(xn, wg_ref[...], preferred_element_type=F32)
    u = jnp.dot(xn, wu_ref[...], preferred_element_type=F32)
    a = (g * jax.nn.sigmoid(g)) * u
    return jnp.dot(a.astype(BF16), wd_ref[...], preferred_element_type=F32)


def _macaron(x, npre_ref, wg_ref, wu_ref, wd_ref, npost_ref):
    y = _swiglu(_rms(x, npre_ref[...]).astype(BF16), wg_ref, wu_ref, wd_ref)
    return x + 0.5 * _rms(y, npost_ref[...])


def _rope_table_kernel(inv_ref, c_ref, sa_ref, sb_ref, *, pos0, same_pos):
    rows = c_ref.shape[0]
    r = lax.broadcasted_iota(jnp.int32, (rows, LANES), 0)
    pos = (jnp.full_like(r, pos0) if same_pos else pos0 + r).astype(F32)
    lane = lax.broadcasted_iota(jnp.int32, (rows, LANES), 1)
    within = lane & (HEAD_DIM - 1)
    ang = pos * inv_ref[...]
    cos, sin = jnp.cos(ang), jnp.sin(ang)
    c_ref[...] = jnp.where(within < ROPE_ROT, cos, 1.0)
    sa_ref[...] = jnp.where(within < ROPE_HALF, -sin, 0.0)
    sb_ref[...] = jnp.where((within >= ROPE_HALF) & (within < ROPE_ROT), sin, 0.0)


def _rope_tables(rows, pos0, same_pos):
    j = jnp.arange(LANES) % ROPE_HALF
    inv = ROPE_THETA ** (-(j.astype(F32)) * 2.0 / ROPE_ROT)
    out = jax.ShapeDtypeStruct((rows, LANES), F32)
    return pl.pallas_call(
        functools.partial(_rope_table_kernel, pos0=pos0, same_pos=same_pos),
        out_shape=(out, out, out), name="rope_tables",
    )(inv.reshape(1, LANES))


def _rope_block(z, c, sa, sb):
    return z * c + pltpu.roll(z, LANES - ROPE_HALF, 1) * sa + pltpu.roll(z, ROPE_HALF, 1) * sb


def _ffn_in_kernel(x_ref, n1_ref, wg_ref, wu_ref, wd_ref, n1p_ref, nm_ref, win_ref,
                   rc_ref, rsa_ref, rsb_ref,
                   h_ref, g_ref, pool_ref, q_ref, k_ref, v_ref, kb_ref, vb_ref,
                   qi_ref, kiwi_ref, kiwib_ref):
    h = _macaron(x_ref[...], n1_ref, wg_ref, wu_ref, wd_ref, n1p_ref)
    h_ref[...] = h
    u = _rms(h, nm_ref[...]).astype(BF16)
    a = jnp.dot(u, win_ref[...], preferred_element_type=F32)
    g_ref[...] = a[:, C_CONV_V:C_CONV_G] * jax.nn.sigmoid(a[:, C_CONV_G:C_POOL])
    pool_ref[...] = a[:, C_POOL:C_Q]

    c, sa, sb = rc_ref[...], rsa_ref[...], rsb_ref[...]
    lane = lax.broadcasted_iota(jnp.int32, c.shape, 1)
    first = lane < HEAD_DIM
    c1, sa1, sb1 = jnp.where(first, c, 1.0), jnp.where(first, sa, 0.0), jnp.where(first, sb, 0.0)

    for i in range(4):
        z = _rope_block(a[:, C_Q + i * LANES:C_Q + (i + 1) * LANES], c, sa, sb)
        q_ref[:, i * LANES:(i + 1) * LANES] = (z * Q_SCALE).astype(BF16)
        z = _rope_block(a[:, C_QI + i * LANES:C_QI + (i + 1) * LANES], c1, sa1, sb1)
        qi_ref[:, i * LANES:(i + 1) * LANES] = z.astype(BF16)
    k = _rope_block(a[:, C_K:C_V], c, sa, sb)
    k_ref[...] = k
    kb_ref[...] = k.astype(BF16)
    v = a[:, C_V:C_QI]
    v_ref[...] = v
    vb_ref[...] = v.astype(BF16)
    kiwi = _rope_block(a[:, C_KIWI:C_END], c1, sa1, sb1)
    kiwi_ref[...] = kiwi
    kiwib_ref[...] = kiwi.astype(BF16)


def _ffn_in(x, lw, layer, rope, tm, rope_blocks):
    n = x.shape[0]
    row = lambda w: pl.BlockSpec((tm, w), lambda i: (i, 0))
    rope_spec = pl.BlockSpec((tm, LANES), lambda i: (i % rope_blocks, 0))
    par = functools.partial(_layer_spec, layer)
    widths = (D_MODEL, CONV_W, POOL_W, 4 * LANES, LANES, LANES, LANES, LANES, 4 * LANES, LANES, LANES)
    dtypes = (F32, F32, F32, BF16, F32, F32, BF16, BF16, BF16, F32, BF16)
    return pl.pallas_call(
        _ffn_in_kernel,
        grid=(n // tm,),
        in_specs=[row(D_MODEL), par((1, D_MODEL)), par((D_MODEL, D_FF)),
                  par((D_MODEL, D_FF)), par((D_FF, D_MODEL)), par((1, D_MODEL)),
                  par((1, D_MODEL)), par((D_MODEL, C_END)),
                  rope_spec, rope_spec, rope_spec],
        out_specs=[row(w) for w in widths],
        out_shape=[jax.ShapeDtypeStruct((n, w), d) for w, d in zip(widths, dtypes)],
        compiler_params=pltpu.CompilerParams(dimension_semantics=("arbitrary",),
                                             vmem_limit_bytes=VMEM_LIMIT),
        name="ffn_in",
    )(x, lw["n1"], lw["wg1"], lw["wu1"], lw["wd1"], lw["n1p"], lw["nm"], lw["win"], *rope)


def _ffn_out_kernel(h_ref, ycp_ref, ya_ref, wocp_ref, woa_ref, nmp_ref,
                    n2_ref, wg_ref, wu_ref, wd_ref, n2p_ref, o_ref):
    y = (jnp.dot(ycp_ref[...], wocp_ref[...], preferred_element_type=F32)
         + jnp.dot(ya_ref[...], woa_ref[...], preferred_element_type=F32))
    h = h_ref[...] + _rms(y, nmp_ref[...])
    o_ref[...] = _macaron(h, n2_ref, wg_ref, wu_ref, wd_ref, n2p_ref)


def _ffn_out(h, ycp, ya, lw, layer, tm):
    n = h.shape[0]
    row = lambda w: pl.BlockSpec((tm, w), lambda i: (i, 0))
    par = functools.partial(_layer_spec, layer)
    return pl.pallas_call(
        _ffn_out_kernel,
        grid=(n // tm,),
        in_specs=[row(D_MODEL), row(CONV_W + POOL_W), row(ATTN_W),
                  par((CONV_W + POOL_W, D_MODEL)), par((ATTN_W, D_MODEL)),
                  par((1, D_MODEL)), par((1, D_MODEL)),
                  par((D_MODEL, D_FF)), par((D_MODEL, D_FF)),
                  par((D_FF, D_MODEL)), par((1, D_MODEL))],
        out_specs=row(D_MODEL),
        out_shape=jax.ShapeDtypeStruct((n, D_MODEL), F32),
        compiler_params=pltpu.CompilerParams(dimension_semantics=("arbitrary",),
                                             vmem_limit_bytes=VMEM_LIMIT),
        name="ffn_out",
    )(h, ycp, ya, lw["wocp"], lw["woa"], lw["nmp"], lw["n2"], lw["wg2"], lw["wu2"], lw["wd2"], lw["n2p"])


CONV_HIST = 32
POOL_HIST = 16


def _mix_seq_kernel(g_ref, x_ref, pc_ref, pp_ref, dw_ref, dwb_ref, lng_ref, lnb_ref, pw_ref,
                    plw_ref, psc_ref, y_ref, nc_ref, np_ref, gbuf, pbuf, *, tt, ttp, pos0, n_t):
    t = pl.program_id(1)
    c0, p0 = CONV_HIST - (CONV_K - 1), POOL_HIST - POOL_BUF

    @pl.when(t == 0)
    def _():
        gbuf[c0:CONV_HIST, :] = pc_ref[0]
        pbuf[p0:POOL_HIST, :] = pp_ref[0]

    if ttp != tt:
        gbuf[CONV_HIST:CONV_HIST + ttp, :] = jnp.zeros((ttp, CONV_W), F32)
        pbuf[POOL_HIST:POOL_HIST + ttp, :] = jnp.zeros((ttp, POOL_W), F32)
    gbuf[CONV_HIST:CONV_HIST + tt, :] = g_ref[0]
    pbuf[POOL_HIST:POOL_HIST + tt, :] = x_ref[0]

    acc = jnp.zeros((ttp, CONV_W), F32) + dwb_ref[...]
    for j in range(CONV_K):
        acc = acc + gbuf[c0 + j:c0 + j + ttp, :] * dw_ref[j:j + 1, :]
    mu = jnp.mean(acc, axis=-1, keepdims=True)
    d = acc - mu
    var = jnp.mean(d * d, axis=-1, keepdims=True)
    yn = d * lax.rsqrt(var + NORM_EPS) * lng_ref[...] + lnb_ref[...]
    yn = yn * jax.nn.sigmoid(yn)
    yc = jnp.dot(yn.astype(BF16), pw_ref[...], preferred_element_type=F32)

    def shifted(j, blk):
        return pbuf[POOL_HIST - j:POOL_HIST - j + ttp, blk * LANES:(blk + 1) * LANES]

    pos = pos0 + t * tt + lax.broadcasted_iota(jnp.int32, (ttp, LANES), 0)
    lane = lax.broadcasted_iota(jnp.int32, (ttp, LANES), 1)
    lo = lane < POOL_GW
    sums = []
    for blk, (w_lo, w_hi) in enumerate(((POOL_WINDOWS[0], POOL_WINDOWS[1]),
                                        (POOL_WINDOWS[2], POOL_WINDOWS[3]))):
        s = shifted(0, blk)
        for j in range(1, w_lo):
            s = s + shifted(j, blk)
        s_lo = s
        for j in range(w_lo, w_hi):
            s = s + shifted(j, blk)
        cnt = jnp.minimum(jnp.where(lo, w_lo, w_hi), pos + 1).astype(F32)
        sums.append(jnp.where(lo, s_lo, s) / cnt - shifted(0, blk))
    pooled = jnp.concatenate(sums, axis=1)
    yp = jnp.dot(pooled.astype(BF16), plw_ref[...], preferred_element_type=F32) * psc_ref[...]

    y_ref[0, :, 0:CONV_W] = yc[0:tt].astype(BF16)
    y_ref[0, :, CONV_W:CONV_W + POOL_W] = yp[0:tt].astype(BF16)

    @pl.when(t == n_t - 1)
    def _():
        nc_ref[0] = gbuf[c0 + tt:CONV_HIST + tt, :]
        np_ref[0] = pbuf[p0 + tt:POOL_HIST + tt, :]

    if n_t > 1:
        @pl.when(t < n_t - 1)
        def _():
            gbuf[0:CONV_HIST, :] = gbuf[tt:tt + CONV_HIST, :]
            pbuf[0:POOL_HIST, :] = pbuf[tt:tt + POOL_HIST, :]


def _mix_seq(g, x, prev_conv, prev_pool, lw, layer, pos0, tt):
    b, t_len, _ = g.shape
    n_t = t_len // tt
    assert n_t * tt == t_len and (n_t == 1 or tt % 8 == 0 and tt >= CONV_HIST)
    ttp = -(-tt // 8) * 8
    tile = lambda w: pl.BlockSpec((1, tt, w), lambda i, j: (i, j, 0))
    state = lambda r, w: pl.BlockSpec((1, r, w), lambda i, j: (i, 0, 0))
    par = functools.partial(_layer_spec, layer)
    return pl.pallas_call(
        functools.partial(_mix_seq_kernel, tt=tt, ttp=ttp, pos0=pos0, n_t=n_t),
        grid=(b, n_t),
        in_specs=[tile(CONV_W), tile(POOL_W), state(CONV_K - 1, CONV_W), state(POOL_BUF, POOL_W),
                  par((CONV_K, CONV_W)), par((1, CONV_W)), par((1, CONV_W)),
                  par((1, CONV_W)), par((CONV_W, CONV_W)),
                  par((POOL_W, POOL_W)), par((1, POOL_W))],
        out_specs=[tile(CONV_W + POOL_W), state(CONV_K - 1, CONV_W), state(POOL_BUF, POOL_W)],
        out_shape=[jax.ShapeDtypeStruct((b, t_len, CONV_W + POOL_W), BF16),
                   jax.ShapeDtypeStruct((b, CONV_K - 1, CONV_W), F32),
                   jax.ShapeDtypeStruct((b, POOL_BUF, POOL_W), F32)],
        scratch_shapes=[pltpu.VMEM((CONV_HIST + ttp, CONV_W), F32),
                        pltpu.VMEM((POOL_HIST + ttp, POOL_W), F32)],
        compiler_params=pltpu.CompilerParams(dimension_semantics=("arbitrary", "arbitrary"),
                                             vmem_limit_bytes=VMEM_LIMIT),
        name="mix_seq",
    )(g, x, prev_conv, prev_pool, lw["dw"], lw["dwb"], lw["lng"], lw["lnb"], lw["pw"],
      lw["plw"], lw["psc"])


COUNT_GROUPS = 16


def _column_count(mask, negate=False):
    n = mask.shape[0] // 8
    ones = (jnp.where(mask, 0.0, 1.0) if negate else jnp.where(mask, 1.0, 0.0)).reshape(n, 8, LANES)
    accs = [ones[g] for g in range(min(COUNT_GROUPS, n))]
    for i in range(len(accs), n):
        accs[i % len(accs)] = accs[i % len(accs)] + ones[i]
    while len(accs) > 1:
        accs = [a + b for a, b in zip(accs[0::2], accs[1::2])] + accs[len(accs) & ~1:]
    return jnp.sum(accs[0], axis=0, keepdims=True)


def _ordered_bits_to_float(key):
    return pltpu.bitcast(key ^ ((key >> 31) & 0x7FFFFFFF), F32)


def _select_bias_t(score_t, valid_t, topk, sc_ref):
    w = score_t.shape[0]
    sc_ref[...] = jnp.where(valid_t, score_t, -jnp.inf)

    def bisect(it, key):
        cand = key + jnp.left_shift(jnp.int32(1), 31 - it)
        cnt = _column_count(sc_ref[...] < _ordered_bits_to_float(cand), negate=True)
        return jnp.where(cnt >= topk, cand, key)

    key = lax.fori_loop(0, 32, bisect, jnp.full((1, LANES), INT32_MIN, jnp.int32))
    thr = _ordered_bits_to_float(key)
    sc = sc_ref[...]
    gt = sc > thr
    need = topk - _column_count(gt)
    eq = sc == thr
    eqf = jnp.where(eq, 1.0, 0.0).astype(BF16)
    r = lax.broadcasted_iota(jnp.int32, (2 * LANES, LANES), 0)
    c = lax.broadcasted_iota(jnp.int32, (2 * LANES, LANES), 1)
    tri = jnp.where((r >= LANES) | (c < r), 1.0, 0.0).astype(BF16)
    offset = jnp.zeros((1, LANES), F32)
    ranks = []
    for ch in range(w // LANES):
        res = jnp.dot(tri, eqf[ch * LANES:(ch + 1) * LANES, :], preferred_element_type=F32)
        ranks.append(res[0:LANES] + offset)
        offset = offset + res[LANES:LANES + 1]
    rank = jnp.concatenate(ranks, axis=0)
    sel = valid_t & (gt | (eq & (rank < need)))
    return jnp.where(sel, 0.0, -jnp.inf)


def _dsa_prompt_kernel(q_ref, qi_ref, wi_ref, k_ref, v_ref, ki_ref, o_ref, sc_ref, *, w, blk0, topk):
    t0 = (blk0 + pl.program_id(1)) * Q_BLOCK
    ki = ki_ref[0]
    wi_t = jnp.transpose(wi_ref[0])
    scale = IDX_DIM ** -0.5 * IDX_HEADS ** -0.5
    score_t = jnp.zeros((w, Q_BLOCK), F32)
    for h in range(IDX_HEADS):
        qh = qi_ref[0, :, h * LANES:(h + 1) * LANES]
        d = lax.dot_general(ki, qh, (((1,), (1,)), ((), ())), preferred_element_type=F32)
        score_t = score_t + jnp.maximum(d, 0.0) * (wi_t[IDX_DIM + h:IDX_DIM + h + 1, :] * scale)
    s_idx = lax.broadcasted_iota(jnp.int32, (w, Q_BLOCK), 0)
    t_idx = t0 + lax.broadcasted_iota(jnp.int32, (w, Q_BLOCK), 1)
    bias = jnp.transpose(_select_bias_t(score_t, s_idx <= t_idx, topk, sc_ref))

    k = k_ref[0]
    v = v_ref[0]
    lane = lax.broadcasted_iota(jnp.int32, (Q_BLOCK, LANES), 1)
    lo = lane < HEAD_DIM
    rep = N_HEADS // N_KV_HEADS
    outs = []
    for g in range(N_KV_HEADS):
        qs = [jnp.where(lo if g == 0 else ~lo, q_ref[0, :, i * LANES:(i + 1) * LANES], 0)
              for i in range(rep)]
        qg = jnp.concatenate(qs, axis=0)
        s = lax.dot_general(qg, k, (((1,), (1,)), ((), ())), preferred_element_type=F32)
        s = s.reshape(rep, Q_BLOCK, w) + bias[None]
        m = jnp.max(s, axis=-1, keepdims=True)
        p = jnp.exp2(s - m)
        l = jnp.sum(p, axis=-1, keepdims=True)
        o = jnp.dot(p.reshape(rep * Q_BLOCK, w).astype(BF16), v, preferred_element_type=F32)
        outs.append(o.reshape(rep, Q_BLOCK, LANES) / l)
    for i in range(rep):
        o_ref[0, :, i * LANES:(i + 1) * LANES] = jnp.where(lo, outs[0][i], outs[1][i]).astype(BF16)


def _dsa_prompt(q, qi, kiwi, kb, vb, kiwib, topk):
    b, t_len, _ = q.shape
    nblk = t_len // Q_BLOCK
    step = min(2, nblk)
    outs = []
    for blk0 in range(0, nblk, step):
        nb = min(step, nblk - blk0)
        w = (blk0 + nb) * Q_BLOCK
        qspec = lambda wd: pl.BlockSpec((1, Q_BLOCK, wd), lambda i, j, blk0=blk0: (i, blk0 + j, 0))
        kspec = pl.BlockSpec((1, w, LANES), lambda i, j: (i, 0, 0))
        outs.append(pl.pallas_call(
            functools.partial(_dsa_prompt_kernel, w=w, blk0=blk0, topk=topk),
            grid=(b, nb),
            in_specs=[qspec(ATTN_W), qspec(IDX_HEADS * LANES), qspec(LANES), kspec, kspec, kspec],
            out_specs=pl.BlockSpec((1, Q_BLOCK, ATTN_W), lambda i, j: (i, j, 0)),
            out_shape=jax.ShapeDtypeStruct((b, nb * Q_BLOCK, ATTN_W), BF16),
            scratch_shapes=[pltpu.VMEM((w, Q_BLOCK), F32)],
            compiler_params=pltpu.CompilerParams(dimension_semantics=("arbitrary", "arbitrary"),
                                                 vmem_limit_bytes=VMEM_LIMIT),
            name=f"dsa_prompt_w{w}",
        )(q, qi, kiwi, kb, vb, kiwib))
    return jnp.concatenate(outs, axis=1) if len(outs) > 1 else outs[0]


def _fetch_pages(pt_ref, cache_ref, layer, buf, sem, seq, slot, n_pages, start):
    def body(j, carry):
        page = pt_ref[seq, j] if start else 0
        cp = pltpu.make_async_copy(cache_ref.at[layer, page],
                                   buf.at[slot, :, pl.ds(pl.multiple_of(j * PAGE_SIZE, PAGE_SIZE), PAGE_SIZE)],
                                   sem.at[slot])
        if start:
            cp.start()
        else:
            cp.wait()
        return carry
    lax.fori_loop(0, n_pages, body, 0)


def _pipelined_pages(pt_ref, caches, layer, bufs, sems, n_pages):
    b, nb = pl.program_id(0), pl.num_programs(0)

    @pl.when(b == 0)
    def _():
        for cache, buf, sem in zip(caches, bufs, sems):
            _fetch_pages(pt_ref, cache, layer, buf, sem, 0, 0, n_pages, True)

    @pl.when(b + 1 < nb)
    def _():
        for cache, buf, sem in zip(caches, bufs, sems):
            _fetch_pages(pt_ref, cache, layer, buf, sem, b + 1, (b + 1) % 2, n_pages, True)

    for cache, buf, sem in zip(caches, bufs, sems):
        _fetch_pages(pt_ref, cache, layer, buf, sem, b, b % 2, n_pages, False)
    return b % 2


def _sample_index_kernel(pt_ref, qi_ref, w_ref, kinew_ref, cache_ref, o_ref, buf, sem, *, layer, n_pages):
    slot = _pipelined_pages(pt_ref, (cache_ref,), layer, (buf,), (sem,), n_pages)
    past = n_pages * PAGE_SIZE
    qi = qi_ref[0]
    wcol = w_ref[0][:, 0:1] * (IDX_DIM ** -0.5 * IDX_HEADS ** -0.5)
    kb = buf[slot].astype(BF16)
    d = jnp.dot(qi[:, 0:IDX_DIM], kb, preferred_element_type=F32)
    o_ref[0, :, 0:past] = jnp.sum(jnp.maximum(d, 0.0) * wcol, axis=0, keepdims=True)
    d_new = jnp.sum(qi.astype(F32) * kinew_ref[0].astype(F32), axis=1, keepdims=True)
    s_new = jnp.sum(jnp.maximum(d_new, 0.0) * wcol, axis=0, keepdims=True)
    lane = lax.broadcasted_iota(jnp.int32, (1, LANES), 1)
    o_ref[0, :, past:past + LANES] = jnp.where(lane == 0, s_new, -jnp.inf)


def _sample_select_kernel(s_ref, o_ref, sc_ref, *, n_valid, topk):
    bs, w = s_ref.shape
    pad = jnp.full((LANES - bs, w), -jnp.inf, F32)
    score_t = jnp.transpose(jnp.concatenate([s_ref[...], pad], axis=0))
    row = lax.broadcasted_iota(jnp.int32, (w, LANES), 0)
    bias_t = _select_bias_t(score_t, row < n_valid, topk, sc_ref)
    o_ref[...] = jnp.transpose(bias_t)[0:bs]


def _sample_attend_kernel(pt_ref, q_ref, bias_ref, knew_ref, vnew_ref, ck_ref, cv_ref, o_ref,
                          kbuf, vbuf, ksem, vsem, *, layer, n_pages):
    slot = _pipelined_pages(pt_ref, (ck_ref, cv_ref), layer, (kbuf, vbuf), (ksem, vsem), n_pages)
    past = n_pages * PAGE_SIZE
    q = q_ref[0]
    s = jnp.dot(q, kbuf[slot].astype(BF16), preferred_element_type=F32)
    s = s + bias_ref[0, :, 0:past]
    s_new = (jnp.sum(q.astype(F32) * knew_ref[0].astype(F32), axis=1, keepdims=True)
             + bias_ref[0, :, past:past + 1])
    m = jnp.maximum(jnp.max(s, axis=1, keepdims=True), s_new)
    p = jnp.exp2(s - m)
    p_new = jnp.exp2(s_new - m)
    l = jnp.sum(p, axis=1, keepdims=True) + p_new
    o = lax.dot_general(p.astype(BF16), vbuf[slot].astype(BF16), (((1,), (1,)), ((), ())),
                        preferred_element_type=F32)
    o = (o + p_new.astype(BF16).astype(F32) * vnew_ref[0].astype(F32)) / l
    lane = lax.broadcasted_iota(jnp.int32, (1, LANES), 1)
    rep = N_HEADS // N_KV_HEADS
    for i in range(rep):
        o_ref[0, :, i * LANES:(i + 1) * LANES] = jnp.where(
            lane < HEAD_DIM, o[i:i + 1], o[i + rep:i + rep + 1]).astype(BF16)


def _dsa_sample(q, qi, kiwi, kb, vb, kiwib, cache_k, cache_v, cache_kidx, page_table, layer, topk):
    bs = q.shape[0]
    n_pages = page_table.shape[1]
    past = n_pages * PAGE_SIZE
    wp = past + LANES
    rows = 16
    rep = N_HEADS // N_KV_HEADS
    lane = jnp.arange(LANES)

    qi16 = jnp.pad(qi.reshape(bs, IDX_HEADS, LANES), ((0, 0), (0, rows - IDX_HEADS), (0, 0)))
    w16 = jnp.pad(jnp.broadcast_to(kiwi[:, IDX_DIM:IDX_DIM + IDX_HEADS, None], (bs, IDX_HEADS, LANES)),
                  ((0, 0), (0, rows - IDX_HEADS), (0, 0)))
    seq = lambda r, wd: pl.BlockSpec((1, r, wd), lambda i, pt: (i, 0, 0))
    any_spec = pl.BlockSpec(memory_space=pl.ANY)
    n_phys = cache_k.shape[1]
    ckidx_t = jnp.swapaxes(cache_kidx, 2, 3)
    ck_t = cache_k.transpose(0, 1, 3, 4, 2).reshape(cache_k.shape[0], n_phys, LANES, PAGE_SIZE)
    cv_t = cache_v.transpose(0, 1, 3, 4, 2).reshape(cache_v.shape[0], n_phys, LANES, PAGE_SIZE)
    scores = pl.pallas_call(
        functools.partial(_sample_index_kernel, layer=layer, n_pages=n_pages),
        grid_spec=pltpu.PrefetchScalarGridSpec(
            num_scalar_prefetch=1, grid=(bs,),
            in_specs=[seq(rows, LANES), seq(rows, LANES), seq(1, LANES), any_spec],
            out_specs=seq(1, wp),
            scratch_shapes=[pltpu.VMEM((2, IDX_DIM, past), F32), pltpu.SemaphoreType.DMA((2,))]),
        out_shape=jax.ShapeDtypeStruct((bs, 1, wp), F32),
        compiler_params=pltpu.CompilerParams(dimension_semantics=("arbitrary",),
                                             vmem_limit_bytes=VMEM_LIMIT),
        name="sample_index",
    )(page_table, qi16, w16, kiwib.reshape(bs, 1, LANES), ckidx_t)

    bias = pl.pallas_call(
        functools.partial(_sample_select_kernel, n_valid=past + 1, topk=topk),
        out_shape=jax.ShapeDtypeStruct((bs, wp), F32),
        scratch_shapes=[pltpu.VMEM((wp, LANES), F32)],
        compiler_params=pltpu.CompilerParams(vmem_limit_bytes=VMEM_LIMIT),
        name="sample_select",
    )(scores.reshape(bs, wp)).reshape(bs, 1, wp)

    q4 = q.reshape(bs, rep, LANES)
    q16 = jnp.concatenate([jnp.where(lane < HEAD_DIM, q4, 0), jnp.where(lane >= HEAD_DIM, q4, 0),
                           jnp.zeros((bs, rows - N_HEADS, LANES), q.dtype)], axis=1)
    return pl.pallas_call(
        functools.partial(_sample_attend_kernel, layer=layer, n_pages=n_pages),
        grid_spec=pltpu.PrefetchScalarGridSpec(
            num_scalar_prefetch=1, grid=(bs,),
            in_specs=[seq(rows, LANES), seq(1, wp), seq(1, LANES), seq(1, LANES), any_spec, any_spec],
            out_specs=seq(1, ATTN_W),
            scratch_shapes=[pltpu.VMEM((2, LANES, past), F32), pltpu.VMEM((2, LANES, past), F32),
                            pltpu.SemaphoreType.DMA((2,)), pltpu.SemaphoreType.DMA((2,))]),
        out_shape=jax.ShapeDtypeStruct((bs, 1, ATTN_W), BF16),
        compiler_params=pltpu.CompilerParams(dimension_semantics=("arbitrary",),
                                             vmem_limit_bytes=VMEM_LIMIT),
        name="sample_attend",
    )(page_table, q16, bias, kb.reshape(bs, 1, LANES), vb.reshape(bs, 1, LANES), ck_t, cv_t).reshape(bs, ATTN_W)


def _head_pairs(x, axis):
    rep = N_HEADS // N_KV_HEADS
    shape = x.shape
    x = x.reshape(shape[:axis] + (N_KV_HEADS, rep, HEAD_DIM) + shape[axis + 1:])
    return jnp.swapaxes(x, axis, axis + 1).reshape(shape)


def _prep_weights(w_in, w_out, pool_w):
    depth = w_in.shape[0]
    o_q, o_k = 2 * CONV_W + POOL_W, 2 * CONV_W + POOL_W + ATTN_W
    o_qi = o_k + 2 * N_KV_HEADS * HEAD_DIM
    o_ki = o_qi + IDX_HEADS * IDX_DIM
    zeros = lambda n: jnp.zeros((depth, D_MODEL, n), w_in.dtype)
    parts = [w_in[..., :o_q], _head_pairs(w_in[..., o_q:o_k], 2), w_in[..., o_k:o_qi]]
    for h in range(IDX_HEADS):
        parts += [w_in[..., o_qi + h * IDX_DIM:o_qi + (h + 1) * IDX_DIM], zeros(LANES - IDX_DIM)]
    parts += [w_in[..., o_ki:], zeros(LANES - IDX_DIM - IDX_HEADS)]
    win = jnp.concatenate(parts, axis=-1).astype(BF16)
    wocp = w_out[:, :CONV_W + POOL_W].astype(BF16)
    woa = _head_pairs(w_out[:, CONV_W + POOL_W:], 1).astype(BF16)
    plw = jnp.zeros((depth, POOL_W, POOL_W), pool_w.dtype)
    for g in range(len(POOL_WINDOWS)):
        plw = plw.at[:, g * POOL_GW:(g + 1) * POOL_GW, g * POOL_GW:(g + 1) * POOL_GW].set(pool_w[:, g])
    return win, wocp, woa, plw.astype(BF16)


def kernel(x_prompt, x_sample, cache_k, cache_v, cache_kidx, state_conv, state_pool, page_table,
           ffn1_norm_pre, ffn1_w_gate, ffn1_w_up, ffn1_w_down, ffn1_norm_post,
           mix_norm_pre, w_in, conv_dw, conv_dw_bias, conv_ln_gain, conv_ln_bias, conv_pw,
           pool_w, pool_scale, w_out, mix_norm_post,
           ffn2_norm_pre, ffn2_w_gate, ffn2_w_up, ffn2_w_down, ffn2_norm_post):
    bp, tp, _ = x_prompt.shape
    bs, ts, _ = x_sample.shape
    assert ts == 1 and tp % Q_BLOCK == 0
    depth = w_in.shape[0]
    past = page_table.shape[1] * PAGE_SIZE

    win, wocp, woa, plw = _prep_weights(w_in, w_out, pool_w)
    vec = lambda a: a.reshape(depth, 1, -1)
    stacks = dict(
        n1=vec(ffn1_norm_pre), wg1=ffn1_w_gate.astype(BF16), wu1=ffn1_w_up.astype(BF16),
        wd1=ffn1_w_down.astype(BF16), n1p=vec(ffn1_norm_post), nm=vec(mix_norm_pre), win=win,
        dw=conv_dw, dwb=vec(conv_dw_bias), lng=vec(conv_ln_gain), lnb=vec(conv_ln_bias),
        pw=conv_pw.astype(BF16), plw=plw, psc=vec(pool_scale), wocp=wocp, woa=woa,
        nmp=vec(mix_norm_post), n2=vec(ffn2_norm_pre), wg2=ffn2_w_gate.astype(BF16),
        wu2=ffn2_w_up.astype(BF16), wd2=ffn2_w_down.astype(BF16), n2p=vec(ffn2_norm_post))

    tm_p = 256 if tp % 256 == 0 else Q_BLOCK
    tt_p = 512 if tp % 512 == 0 else Q_BLOCK
    rope_p = _rope_tables(tp, 0, False)
    rope_s = _rope_tables(bs * ts, past, True)
    topk_p = min(INDEX_TOPK, tp // 4)
    topk_s = min(INDEX_TOPK, (past + ts) // 4)
    zc = jnp.zeros((bp, CONV_K - 1, CONV_W), F32)
    zp = jnp.zeros((bp, POOL_BUF, POOL_W), F32)

    hp = x_prompt.reshape(bp * tp, D_MODEL)
    hs = x_sample.reshape(bs * ts, D_MODEL)
    acc_p, acc_s = [], []
    lw = stacks
    for l in range(depth):
        hp, g, xpool, q, k, v, kb, vb, qi, kiwi, kiwib = _ffn_in(hp, lw, l, rope_p, tm_p, tp // tm_p)
        seq = lambda a: a.reshape(bp, tp, a.shape[-1])
        ycp, nconv, npool = _mix_seq(seq(g), seq(xpool), zc, zp, lw, l, 0, tt_p)
        ya = _dsa_prompt(seq(q), seq(qi), seq(kiwi), seq(kb), seq(vb), seq(kiwib), topk_p)
        hp = _ffn_out(hp, ycp.reshape(bp * tp, -1), ya.reshape(bp * tp, -1), lw, l, tm_p)
        acc_p.append((k.reshape(bp, tp, N_KV_HEADS, HEAD_DIM), v.reshape(bp, tp, N_KV_HEADS, HEAD_DIM),
                      kiwi[:, :IDX_DIM].reshape(bp, tp, IDX_DIM), nconv, npool))
        hs, g, xpool, q, k, v, kb, vb, qi, kiwi, kiwib = _ffn_in(hs, lw, l, rope_s, bs * ts, 1)
        seq = lambda a: a.reshape(bs, ts, a.shape[-1])
        ycp, nconv, npool = _mix_seq(seq(g), seq(xpool), state_conv[l], state_pool[l], lw, l, past, ts)
        ya = _dsa_sample(q, qi, kiwi, kb, vb, kiwib, cache_k, cache_v, cache_kidx, page_table, l, topk_s)
        hs = _ffn_out(hs, ycp.reshape(bs * ts, -1), ya, lw, l, bs * ts)
        acc_s.append((k.reshape(bs, ts, N_KV_HEADS, HEAD_DIM), v.reshape(bs, ts, N_KV_HEADS, HEAD_DIM),
                      kiwi[:, :IDX_DIM].reshape(bs, ts, IDX_DIM), nconv, npool))

    stack = lambda acc, i: jnp.stack([a[i] for a in acc])
    return ((hp.reshape(bp, tp, D_MODEL), hs.reshape(bs, ts, D_MODEL))
            + tuple(stack(acc_p, i) for i in range(5)) + tuple(stack(acc_s, i) for i in range(5)))
```

```python
import functools

import jax
import jax.numpy as jnp
from jax import lax
from jax.experimental import pallas as pl
from jax.experimental.pallas import tpu as pltpu

D_MODEL = 1024
PAGE_SIZE = 128
CONV_W = D_MODEL // 4
CONV_K = 31
POOL_W = D_MODEL // 4
POOL_WINDOWS = (2, 4, 8, 16)
POOL_GW = POOL_W // len(POOL_WINDOWS)
POOL_BUF = max(POOL_WINDOWS) - 1
HEAD_DIM = 64
ATTN_W = D_MODEL - CONV_W - POOL_W
N_HEADS = ATTN_W // HEAD_DIM
N_KV_HEADS = 2
IDX_HEADS = 4
IDX_DIM = 64
INDEX_TOPK = 256
ROPE_THETA = 500000.0
ROPE_ROT = HEAD_DIM // 4
ROPE_HALF = ROPE_ROT // 2
Q_BLOCK = 128
D_FF = 2816
NORM_EPS = 1e-6

Q_SCALE = HEAD_DIM ** -0.5 * 1.4426950408889634

LANES = 128
V7X_VMEM_BYTES = 64 * 1024 * 1024
VMEM_LIMIT = V7X_VMEM_BYTES - 8 * 1024 * 1024
INT32_MIN = -(2 ** 31)

C_CONV_V, C_CONV_G, C_POOL, C_Q, C_K, C_V, C_QI, C_KIWI, C_END = (
    0, 2 * LANES, 4 * LANES, 6 * LANES, 10 * LANES, 11 * LANES, 12 * LANES, 16 * LANES, 17 * LANES)

F32 = jnp.float32
BF16 = jnp.bfloat16


def _layer_spec(layer, shape):
    return pl.BlockSpec((None,) + shape, lambda *_: (layer,) + (0,) * len(shape),
                        pipeline_mode=pl.Buffered(1))


def _rms(x, g):
    ms = jnp.mean(x * x, axis=-1, keepdims=True)
    return x * lax.rsqrt(ms + NORM_EPS) * g


def _swiglu(xn, wg_ref, wu_ref, wd_ref):
    g = jnp.dot(xn, wg_ref[...], preferred_element_type=F32)
    u = jnp.dot(xn, wu_ref[...], preferred_element_type=F32)
    a = (g * jax.nn.sigmoid(g)) * u
    return jnp.dot(a.astype(BF16), wd_ref[...], preferred_element_type=F32)


def _macaron(x, npre_ref, wg_ref, wu_ref, wd_ref, npost_ref):
    y = _swiglu(_rms(x, npre_ref[...]).astype(BF16), wg_ref, wu_ref, wd_ref)
    return x + 0.5 * _rms(y, npost_ref[...])


def _rope_table_kernel(inv_ref, c_ref, sa_ref, sb_ref, *, pos0, same_pos):
    rows = c_ref.shape[0]
    r = lax.broadcasted_iota(jnp.int32, (rows, LANES), 0)
    pos = (jnp.full_like(r, pos0) if same_pos else pos0 + r).astype(F32)
    lane = lax.broadcasted_iota(jnp.int32, (rows, LANES), 1)
    within = lane & (HEAD_DIM - 1)
    ang = pos * inv_ref[...]
    cos, sin = jnp.cos(ang), jnp.sin(ang)
    c_ref[...] = jnp.where(within < ROPE_ROT, cos, 1.0)
    sa_ref[...] = jnp.where(within < ROPE_HALF, -sin, 0.0)
    sb_ref[...] = jnp.where((within >= ROPE_HALF) & (within < ROPE_ROT), sin, 0.0)


def _rope_tables(rows, pos0, same_pos):
    j = jnp.arange(LANES) % ROPE_HALF
    inv = ROPE_THETA ** (-(j.astype(F32)) * 2.0 / ROPE_ROT)
    out = jax.ShapeDtypeStruct((rows, LANES), F32)
    return pl.pallas_call(
        functools.partial(_rope_table_kernel, pos0=pos0, same_pos=same_pos),
        out_shape=(out, out, out), name="rope_tables",
    )(inv.reshape(1, LANES))


def _rope_block(z, c, sa, sb):
    return z * c + pltpu.roll(z, LANES - ROPE_HALF, 1) * sa + pltpu.roll(z, ROPE_HALF, 1) * sb


def _ffn_in_kernel(x_ref, n1_ref, wg_ref, wu_ref, wd_ref, n1p_ref, nm_ref, win_ref,
                   rc_ref, rsa_ref, rsb_ref,
                   h_ref, g_ref, pool_ref, q_ref, k_ref, v_ref, kb_ref, vb_ref,
                   qi_ref, kiwi_ref, kiwib_ref):
    h = _macaron(x_ref[...], n1_ref, wg_ref, wu_ref, wd_ref, n1p_ref)
    h_ref[...] = h
    u = _rms(h, nm_ref[...]).astype(BF16)
    a = jnp.dot(u, win_ref[...], preferred_element_type=F32)
    g_ref[...] = a[:, C_CONV_V:C_CONV_G] * jax.nn.sigmoid(a[:, C_CONV_G:C_POOL])
    pool_ref[...] = a[:, C_POOL:C_Q]

    c, sa, sb = rc_ref[...], rsa_ref[...], rsb_ref[...]
    lane = lax.broadcasted_iota(jnp.int32, c.shape, 1)
    first = lane < HEAD_DIM
    c1, sa1, sb1 = jnp.where(first, c, 1.0), jnp.where(first, sa, 0.0), jnp.where(first, sb, 0.0)

    for i in range(4):
        z = _rope_block(a[:, C_Q + i * LANES:C_Q + (i + 1) * LANES], c, sa, sb)
        q_ref[:, i * LANES:(i + 1) * LANES] = (z * Q_SCALE).astype(BF16)
        z = _rope_block(a[:, C_QI + i * LANES:C_QI + (i + 1) * LANES], c1, sa1, sb1)
        qi_ref[:, i * LANES:(i + 1) * LANES] = z.astype(BF16)
    k = _rope_block(a[:, C_K:C_V], c, sa, sb)
    k_ref[...] = k
    kb_ref[...] = k.astype(BF16)
    v = a[:, C_V:C_QI]
    v_ref[...] = v
    vb_ref[...] = v.astype(BF16)
    kiwi = _rope_block(a[:, C_KIWI:C_END], c1, sa1, sb1)
    kiwi_ref[...] = kiwi
    kiwib_ref[...] = kiwi.astype(BF16)


def _ffn_in(x, lw, layer, rope, tm, rope_blocks):
    n = x.shape[0]
    row = lambda w: pl.BlockSpec((tm, w), lambda i: (i, 0))
    rope_spec = pl.BlockSpec((tm, LANES), lambda i: (i % rope_blocks, 0))
    par = functools.partial(_layer_spec, layer)
    widths = (D_MODEL, CONV_W, POOL_W, 4 * LANES, LANES, LANES, LANES, LANES, 4 * LANES, LANES, LANES)
    dtypes = (F32, F32, F32, BF16, F32, F32, BF16, BF16, BF16, F32, BF16)
    return pl.pallas_call(
        _ffn_in_kernel,
        grid=(n // tm,),
        in_specs=[row(D_MODEL), par((1, D_MODEL)), par((D_MODEL, D_FF)),
                  par((D_MODEL, D_FF)), par((D_FF, D_MODEL)), par((1, D_MODEL)),
                  par((1, D_MODEL)), par((D_MODEL, C_END)),
                  rope_spec, rope_spec, rope_spec],
        out_specs=[row(w) for w in widths],
        out_shape=[jax.ShapeDtypeStruct((n, w), d) for w, d in zip(widths, dtypes)],
        compiler_params=pltpu.CompilerParams(dimension_semantics=("arbitrary",),
                                             vmem_limit_bytes=VMEM_LIMIT),
        name="ffn_in",
    )(x, lw["n1"], lw["wg1"], lw["wu1"], lw["wd1"], lw["n1p"], lw["nm"], lw["win"], *rope)


def _ffn_out_kernel(h_ref, ycp_ref, ya_ref, wocp_ref, woa_ref, nmp_ref,
                    n2_ref, wg_ref, wu_ref, wd_ref, n2p_ref, o_ref):
    y = (jnp.dot(ycp_ref[...], wocp_ref[...], preferred_element_type=F32)
         + jnp.dot(ya_ref[...], woa_ref[...], preferred_element_type=F32))
    h = h_ref[...] + _rms(y, nmp_ref[...])
    o_ref[...] = _macaron(h, n2_ref, wg_ref, wu_ref, wd_ref, n2p_ref)


def _ffn_out(h, ycp, ya, lw, layer, tm):
    n = h.shape[0]
    row = lambda w: pl.BlockSpec((tm, w), lambda i: (i, 0))
    par = functools.partial(_layer_spec, layer)
    return pl.pallas_call(
        _ffn_out_kernel,
        grid=(n // tm,),
        in_specs=[row(D_MODEL), row(CONV_W + POOL_W), row(ATTN_W),
                  par((CONV_W + POOL_W, D_MODEL)), par((ATTN_W, D_MODEL)),
                  par((1, D_MODEL)), par((1, D_MODEL)),
                  par((D_MODEL, D_FF)), par((D_MODEL, D_FF)),
                  par((D_FF, D_MODEL)), par((1, D_MODEL))],
        out_specs=row(D_MODEL),
        out_shape=jax.ShapeDtypeStruct((n, D_MODEL), F32),
        compiler_params=pltpu.CompilerParams(dimension_semantics=("arbitrary",),
                                             vmem_limit_bytes=VMEM_LIMIT),
        name="ffn_out",
    )(h, ycp, ya, lw["wocp"], lw["woa"], lw["nmp"], lw["n2"], lw["wg2"], lw["wu2"], lw["wd2"], lw["n2p"])


CONV_HIST = 32
POOL_HIST = 16


def _mix_seq_kernel(g_ref, x_ref, pc_ref, pp_ref, dw_ref, dwb_ref, lng_ref, lnb_ref, pw_ref,
                    plw_ref, psc_ref, y_ref, nc_ref, np_ref, gbuf, pbuf, sbuf, *, tt, ttp, pos0, n_t):
    t = pl.program_id(1)
    c0, p0 = CONV_HIST - (CONV_K - 1), POOL_HIST - POOL_BUF

    @pl.when(t == 0)
    def _():
        gbuf[c0:CONV_HIST, :] = pc_ref[0]
        pbuf[p0:POOL_HIST, :] = pp_ref[0]

    if ttp != tt:
        gbuf[CONV_HIST:CONV_HIST + ttp, :] = jnp.zeros((ttp, CONV_W), F32)
        pbuf[POOL_HIST:POOL_HIST + ttp, :] = jnp.zeros((ttp, POOL_W), F32)
    gbuf[CONV_HIST:CONV_HIST + tt, :] = g_ref[0]
    pbuf[POOL_HIST:POOL_HIST + tt, :] = x_ref[0]

    acc = jnp.zeros((ttp, CONV_W), F32) + dwb_ref[...]
    for b in range(8):
        taps = range(b, CONV_K, 8)
        span = ttp + taps[-1] - b
        sbuf[0:span, :] = gbuf[c0 + b:c0 + b + span, :]
        for j in taps:
            acc = acc + sbuf[j - b:j - b + ttp, :] * dw_ref[j:j + 1, :]
    mu = jnp.mean(acc, axis=-1, keepdims=True)
    d = acc - mu
    var = jnp.mean(d * d, axis=-1, keepdims=True)
    yn = d * lax.rsqrt(var + NORM_EPS) * lng_ref[...] + lnb_ref[...]
    yn = yn * jax.nn.sigmoid(yn)
    yc = jnp.dot(yn.astype(BF16), pw_ref[...], preferred_element_type=F32)

    def shifted(j, blk):
        return pbuf[POOL_HIST - j:POOL_HIST - j + ttp, blk * LANES:(blk + 1) * LANES]

    pos = pos0 + t * tt + lax.broadcasted_iota(jnp.int32, (ttp, LANES), 0)
    lane = lax.broadcasted_iota(jnp.int32, (ttp, LANES), 1)
    lo = lane < POOL_GW
    sums = []
    for blk, (w_lo, w_hi) in enumerate(((POOL_WINDOWS[0], POOL_WINDOWS[1]),
                                        (POOL_WINDOWS[2], POOL_WINDOWS[3]))):
        s = shifted(0, blk)
        for j in range(1, w_lo):
            s = s + shifted(j, blk)
        s_lo = s
        for j in range(w_lo, w_hi):
            s = s + shifted(j, blk)
        cnt = jnp.minimum(jnp.where(lo, w_lo, w_hi), pos + 1).astype(F32)
        sums.append(jnp.where(lo, s_lo, s) / cnt - shifted(0, blk))
    pooled = jnp.concatenate(sums, axis=1)
    yp = jnp.dot(pooled.astype(BF16), plw_ref[...], preferred_element_type=F32) * psc_ref[...]

    y_ref[0, :, 0:CONV_W] = yc[0:tt].astype(BF16)
    y_ref[0, :, CONV_W:CONV_W + POOL_W] = yp[0:tt].astype(BF16)

    @pl.when(t == n_t - 1)
    def _():
        nc_ref[0] = gbuf[c0 + tt:CONV_HIST + tt, :]
        np_ref[0] = pbuf[p0 + tt:POOL_HIST + tt, :]

    if n_t > 1:
        @pl.when(t < n_t - 1)
        def _():
            gbuf[0:CONV_HIST, :] = gbuf[tt:tt + CONV_HIST, :]
            pbuf[0:POOL_HIST, :] = pbuf[tt:tt + POOL_HIST, :]


def _mix_seq(g, x, prev_conv, prev_pool, lw, layer, pos0, tt):
    b, t_len, _ = g.shape
    n_t = t_len // tt
    assert n_t * tt == t_len and (n_t == 1 or tt % 8 == 0 and tt >= CONV_HIST)
    ttp = -(-tt // 8) * 8
    tile = lambda w: pl.BlockSpec((1, tt, w), lambda i, j: (i, j, 0))
    state = lambda r, w: pl.BlockSpec((1, r, w), lambda i, j: (i, 0, 0))
    par = functools.partial(_layer_spec, layer)
    return pl.pallas_call(
        functools.partial(_mix_seq_kernel, tt=tt, ttp=ttp, pos0=pos0, n_t=n_t),
        grid=(b, n_t),
        in_specs=[tile(CONV_W), tile(POOL_W), state(CONV_K - 1, CONV_W), state(POOL_BUF, POOL_W),
                  par((CONV_K, CONV_W)), par((1, CONV_W)), par((1, CONV_W)),
                  par((1, CONV_W)), par((CONV_W, CONV_W)),
                  par((POOL_W, POOL_W)), par((1, POOL_W))],
        out_specs=[tile(CONV_W + POOL_W), state(CONV_K - 1, CONV_W), state(POOL_BUF, POOL_W)],
        out_shape=[jax.ShapeDtypeStruct((b, t_len, CONV_W + POOL_W), BF16),
                   jax.ShapeDtypeStruct((b, CONV_K - 1, CONV_W), F32),
                   jax.ShapeDtypeStruct((b, POOL_BUF, POOL_W), F32)],
        scratch_shapes=[pltpu.VMEM((CONV_HIST + ttp, CONV_W), F32),
                        pltpu.VMEM((POOL_HIST + ttp, POOL_W), F32),
                        pltpu.VMEM((CONV_HIST + ttp, CONV_W), F32)],
        compiler_params=pltpu.CompilerParams(dimension_semantics=("arbitrary", "arbitrary"),
                                             vmem_limit_bytes=VMEM_LIMIT),
        name="mix_seq",
    )(g, x, prev_conv, prev_pool, lw["dw"], lw["dwb"], lw["lng"], lw["lnb"], lw["pw"],
      lw["plw"], lw["psc"])


COUNT_GROUPS = 8


def _column_count(mask, negate=False):
    n = mask.shape[0] // 8
    mask = mask.reshape(n, 8, LANES)
    accs = [jnp.zeros((8, LANES), F32) for _ in range(min(COUNT_GROUPS, n))]
    for i in range(n):
        a = accs[i % len(accs)]
        accs[i % len(accs)] = jnp.where(mask[i], a, a + 1.0) if negate else jnp.where(mask[i], a + 1.0, a)
    while len(accs) > 1:
        accs = [a + b for a, b in zip(accs[0::2], accs[1::2])] + accs[len(accs) & ~1:]
    return jnp.sum(accs[0], axis=0, keepdims=True)


def _ordered_bits_to_float(key):
    return pltpu.bitcast(key ^ ((key >> 31) & 0x7FFFFFFF), F32)


def _select_bias_t(score_t, valid_t, topk, sc_ref, unroll=False):
    w = score_t.shape[0]
    sc_ref[...] = jnp.where(valid_t, score_t, -jnp.inf)

    def bisect(it, key):
        cand = key + jnp.left_shift(jnp.int32(1), 31 - it)
        cnt = _column_count(sc_ref[...] < _ordered_bits_to_float(cand), negate=True)
        return jnp.where(cnt >= topk, cand, key)

    key = lax.fori_loop(0, 32, bisect, jnp.full((1, LANES), INT32_MIN, jnp.int32), unroll=unroll)
    thr = _ordered_bits_to_float(key)
    sc = sc_ref[...]
    gt = sc > thr
    need = topk - _column_count(gt)
    eq = sc == thr
    eqf = jnp.where(eq, 1.0, 0.0).astype(BF16)
    r = lax.broadcasted_iota(jnp.int32, (2 * LANES, LANES), 0)
    c = lax.broadcasted_iota(jnp.int32, (2 * LANES, LANES), 1)
    tri = jnp.where((r >= LANES) | (c < r), 1.0, 0.0).astype(BF16)
    offset = jnp.zeros((1, LANES), F32)
    ranks = []
    n_ch = w // LANES
    for ch0 in range(0, n_ch, 2):
        chunks = range(ch0, min(ch0 + 2, n_ch))
        rhs = jnp.concatenate([eqf[ch * LANES:(ch + 1) * LANES, :] for ch in chunks], axis=1)
        res2 = jnp.dot(tri, rhs, preferred_element_type=F32)
        for ch in chunks:
            res = res2[:, (ch - ch0) * LANES:(ch - ch0 + 1) * LANES]
            ranks.append(res[0:LANES] + offset)
            offset = offset + res[LANES:LANES + 1]
    rank = jnp.concatenate(ranks, axis=0)
    sel = valid_t & (gt | (eq & (rank < need)))
    return jnp.where(sel, 0.0, -jnp.inf)


def _dsa_prompt_kernel(q_ref, qi_ref, wi_ref, k_ref, v_ref, ki_ref, o_ref, sc_ref, *, w, blk0, topk):
    t0 = (blk0 + pl.program_id(1)) * Q_BLOCK
    ki = ki_ref[0]
    wi_t = jnp.transpose(wi_ref[0])
    scale = IDX_DIM ** -0.5 * IDX_HEADS ** -0.5
    score_t = jnp.zeros((w, Q_BLOCK), F32)
    for h0 in range(0, IDX_HEADS, 2):
        qh = jnp.concatenate([qi_ref[0, :, h * LANES:(h + 1) * LANES] for h in (h0, h0 + 1)], axis=0)
        d2 = lax.dot_general(ki, qh, (((1,), (1,)), ((), ())), preferred_element_type=F32)
        for h in (h0, h0 + 1):
            d = d2[:, (h - h0) * Q_BLOCK:(h - h0 + 1) * Q_BLOCK]
            score_t = score_t + jnp.maximum(d, 0.0) * (wi_t[IDX_DIM + h:IDX_DIM + h + 1, :] * scale)
    s_idx = lax.broadcasted_iota(jnp.int32, (w, Q_BLOCK), 0)
    t_idx = t0 + lax.broadcasted_iota(jnp.int32, (w, Q_BLOCK), 1)
    bias = jnp.transpose(_select_bias_t(score_t, s_idx <= t_idx, topk, sc_ref))

    k = k_ref[0]
    v = v_ref[0]
    lane = lax.broadcasted_iota(jnp.int32, (Q_BLOCK, LANES), 1)
    lo = lane < HEAD_DIM
    rep = N_HEADS // N_KV_HEADS
    outs = []
    for g in range(N_KV_HEADS):
        qs = [jnp.where(lo if g == 0 else ~lo, q_ref[0, :, i * LANES:(i + 1) * LANES], 0)
              for i in range(rep)]
        qg = jnp.concatenate(qs, axis=0)
        s = lax.dot_general(qg, k, (((1,), (1,)), ((), ())), preferred_element_type=F32)
        s = s.reshape(rep, Q_BLOCK, w) + bias[None]
        m = jnp.max(s, axis=-1, keepdims=True)
        p = jnp.exp2(s - m)
        l = jnp.sum(p, axis=-1, keepdims=True)
        o = jnp.dot(p.reshape(rep * Q_BLOCK, w).astype(BF16), v, preferred_element_type=F32)
        outs.append(o.reshape(rep, Q_BLOCK, LANES) / l)
    for i in range(rep):
        o_ref[0, :, i * LANES:(i + 1) * LANES] = jnp.where(lo, outs[0][i], outs[1][i]).astype(BF16)


def _dsa_prompt(q, qi, kiwi, kb, vb, kiwib, topk):
    b, t_len, _ = q.shape
    nblk = t_len // Q_BLOCK
    step = min(2, nblk)
    outs = []
    for blk0 in range(0, nblk, step):
        nb = min(step, nblk - blk0)
        w = (blk0 + nb) * Q_BLOCK
        qspec = lambda wd: pl.BlockSpec((1, Q_BLOCK, wd), lambda i, j, blk0=blk0: (i, blk0 + j, 0))
        kspec = pl.BlockSpec((1, w, LANES), lambda i, j: (i, 0, 0))
        outs.append(pl.pallas_call(
            functools.partial(_dsa_prompt_kernel, w=w, blk0=blk0, topk=topk),
            grid=(b, nb),
            in_specs=[qspec(ATTN_W), qspec(IDX_HEADS * LANES), qspec(LANES), kspec, kspec, kspec],
            out_specs=pl.BlockSpec((1, Q_BLOCK, ATTN_W), lambda i, j: (i, j, 0)),
            out_shape=jax.ShapeDtypeStruct((b, nb * Q_BLOCK, ATTN_W), BF16),
            scratch_shapes=[pltpu.VMEM((w, Q_BLOCK), F32)],
            compiler_params=pltpu.CompilerParams(dimension_semantics=("arbitrary", "arbitrary"),
                                                 vmem_limit_bytes=VMEM_LIMIT),
            name=f"dsa_prompt_w{w}",
        )(q, qi, kiwi, kb, vb, kiwib))
    return jnp.concatenate(outs, axis=1) if len(outs) > 1 else outs[0]


def _fetch_pages(pt_ref, cache_ref, layer, buf, sem, seq, slot, n_pages, start):
    def body(j, carry):
        page = pt_ref[seq, j] if start else 0
        cp = pltpu.make_async_copy(cache_ref.at[layer, page],
                                   buf.at[slot, :, pl.ds(pl.multiple_of(j * PAGE_SIZE, PAGE_SIZE), PAGE_SIZE)],
                                   sem.at[slot])
        if start:
            cp.start()
        else:
            cp.wait()
        return carry
    lax.fori_loop(0, n_pages, body, 0)


def _pipelined_pages(pt_ref, caches, layer, bufs, sems, n_pages):
    b, nb = pl.program_id(0), pl.num_programs(0)

    @pl.when(b == 0)
    def _():
        for cache, buf, sem in zip(caches, bufs, sems):
            _fetch_pages(pt_ref, cache, layer, buf, sem, 0, 0, n_pages, True)

    @pl.when(b + 1 < nb)
    def _():
        for cache, buf, sem in zip(caches, bufs, sems):
            _fetch_pages(pt_ref, cache, layer, buf, sem, b + 1, (b + 1) % 2, n_pages, True)

    for cache, buf, sem in zip(caches, bufs, sems):
        _fetch_pages(pt_ref, cache, layer, buf, sem, b, b % 2, n_pages, False)
    return b % 2


def _sample_index_kernel(pt_ref, qi_ref, w_ref, kinew_ref, cache_ref, o_ref, buf, sem, *, layer, n_pages):
    slot = _pipelined_pages(pt_ref, (cache_ref,), layer, (buf,), (sem,), n_pages)
    past = n_pages * PAGE_SIZE
    qi = qi_ref[0]
    wcol = w_ref[0][:, 0:1] * (IDX_DIM ** -0.5 * IDX_HEADS ** -0.5)
    kb = buf[slot].astype(BF16)
    d = jnp.dot(qi[:, 0:IDX_DIM], kb, preferred_element_type=F32)
    o_ref[0, :, 0:past] = jnp.sum(jnp.maximum(d, 0.0) * wcol, axis=0, keepdims=True)
    d_new = jnp.sum(qi.astype(F32) * kinew_ref[0].astype(F32), axis=1, keepdims=True)
    s_new = jnp.sum(jnp.maximum(d_new, 0.0) * wcol, axis=0, keepdims=True)
    lane = lax.broadcasted_iota(jnp.int32, (1, LANES), 1)
    o_ref[0, :, past:past + LANES] = jnp.where(lane == 0, s_new, -jnp.inf)


def _sample_select_kernel(s_ref, o_ref, sc_ref, *, n_valid, topk):
    bs, w = s_ref.shape
    pad = jnp.full((LANES - bs, w), -jnp.inf, F32)
    score_t = jnp.transpose(jnp.concatenate([s_ref[...], pad], axis=0))
    row = lax.broadcasted_iota(jnp.int32, (w, LANES), 0)
    bias_t = _select_bias_t(score_t, row < n_valid, topk, sc_ref)
    o_ref[...] = jnp.transpose(bias_t)[0:bs]


def _sample_attend_kernel(pt_ref, q_ref, bias_ref, knew_ref, vnew_ref, ck_ref, cv_ref, o_ref,
                          kbuf, vbuf, ksem, vsem, *, layer, n_pages):
    slot = _pipelined_pages(pt_ref, (ck_ref, cv_ref), layer, (kbuf, vbuf), (ksem, vsem), n_pages)
    past = n_pages * PAGE_SIZE
    q = q_ref[0]
    s = jnp.dot(q, kbuf[slot].astype(BF16), preferred_element_type=F32)
    s = s + bias_ref[0, :, 0:past]
    s_new = (jnp.sum(q.astype(F32) * knew_ref[0].astype(F32), axis=1, keepdims=True)
             + bias_ref[0, :, past:past + 1])
    m = jnp.maximum(jnp.max(s, axis=1, keepdims=True), s_new)
    p = jnp.exp2(s - m)
    p_new = jnp.exp2(s_new - m)
    l = jnp.sum(p, axis=1, keepdims=True) + p_new
    o = lax.dot_general(p.astype(BF16), vbuf[slot].astype(BF16), (((1,), (1,)), ((), ())),
                        preferred_element_type=F32)
    o = (o + p_new.astype(BF16).astype(F32) * vnew_ref[0].astype(F32)) / l
    lane = lax.broadcasted_iota(jnp.int32, (1, LANES), 1)
    rep = N_HEADS // N_KV_HEADS
    for i in range(rep):
        o_ref[0, :, i * LANES:(i + 1) * LANES] = jnp.where(
            lane < HEAD_DIM, o[i:i + 1], o[i + rep:i + rep + 1]).astype(BF16)


def _dsa_sample(q, qi, kiwi, kb, vb, kiwib, cache_k, cache_v, cache_kidx, page_table, layer, topk):
    bs = q.shape[0]
    n_pages = page_table.shape[1]
    past = n_pages * PAGE_SIZE
    wp = past + LANES
    rows = 16
    rep = N_HEADS // N_KV_HEADS
    lane = jnp.arange(LANES)

    qi16 = jnp.pad(qi.reshape(bs, IDX_HEADS, LANES), ((0, 0), (0, rows - IDX_HEADS), (0, 0)))
    w16 = jnp.pad(jnp.broadcast_to(kiwi[:, IDX_DIM:IDX_DIM + IDX_HEADS, None], (bs, IDX_HEADS, LANES)),
                  ((0, 0), (0, rows - IDX_HEADS), (0, 0)))
    seq = lambda r, wd: pl.BlockSpec((1, r, wd), lambda i, pt: (i, 0, 0))
    any_spec = pl.BlockSpec(memory_space=pl.ANY)
    n_phys = cache_k.shape[1]
    ckidx_t = jnp.swapaxes(cache_kidx, 2, 3)
    ck_t = cache_k.transpose(0, 1, 3, 4, 2).reshape(cache_k.shape[0], n_phys, LANES, PAGE_SIZE)
    cv_t = cache_v.transpose(0, 1, 3, 4, 2).reshape(cache_v.shape[0], n_phys, LANES, PAGE_SIZE)
    scores = pl.pallas_call(
        functools.partial(_sample_index_kernel, layer=layer, n_pages=n_pages),
        grid_spec=pltpu.PrefetchScalarGridSpec(
            num_scalar_prefetch=1, grid=(bs,),
            in_specs=[seq(rows, LANES), seq(rows, LANES), seq(1, LANES), any_spec],
            out_specs=seq(1, wp),
            scratch_shapes=[pltpu.VMEM((2, IDX_DIM, past), F32), pltpu.SemaphoreType.DMA((2,))]),
        out_shape=jax.ShapeDtypeStruct((bs, 1, wp), F32),
        compiler_params=pltpu.CompilerParams(dimension_semantics=("arbitrary",),
                                             vmem_limit_bytes=VMEM_LIMIT),
        name="sample_index",
    )(page_table, qi16, w16, kiwib.reshape(bs, 1, LANES), ckidx_t)

    bias = pl.pallas_call(
        functools.partial(_sample_select_kernel, n_valid=past + 1, topk=topk),
        out_shape=jax.ShapeDtypeStruct((bs, wp), F32),
        scratch_shapes=[pltpu.VMEM((wp, LANES), F32)],
        compiler_params=pltpu.CompilerParams(vmem_limit_bytes=VMEM_LIMIT),
        name="sample_select",
    )(scores.reshape(bs, wp)).reshape(bs, 1, wp)

    q4 = q.reshape(bs, rep, LANES)
    q16 = jnp.concatenate([jnp.where(lane < HEAD_DIM, q4, 0), jnp.where(lane >= HEAD_DIM, q4, 0),
                           jnp.zeros((bs, rows - N_HEADS, LANES), q.dtype)], axis=1)
    return pl.pallas_call(
        functools.partial(_sample_attend_kernel, layer=layer, n_pages=n_pages),
        grid_spec=pltpu.PrefetchScalarGridSpec(
            num_scalar_prefetch=1, grid=(bs,),
            in_specs=[seq(rows, LANES), seq(1, wp), seq(1, LANES), seq(1, LANES), any_spec, any_spec],
            out_specs=seq(1, ATTN_W),
            scratch_shapes=[pltpu.VMEM((2, LANES, past), F32), pltpu.VMEM((2, LANES, past), F32),
                            pltpu.SemaphoreType.DMA((2,)), pltpu.SemaphoreType.DMA((2,))]),
        out_shape=jax.ShapeDtypeStruct((bs, 1, ATTN_W), BF16),
        compiler_params=pltpu.CompilerParams(dimension_semantics=("arbitrary",),
                                             vmem_limit_bytes=VMEM_LIMIT),
        name="sample_attend",
    )(page_table, q16, bias, kb.reshape(bs, 1, LANES), vb.reshape(bs, 1, LANES), ck_t, cv_t).reshape(bs, ATTN_W)


def _head_pairs(x, axis):
    rep = N_HEADS // N_KV_HEADS
    shape = x.shape
    x = x.reshape(shape[:axis] + (N_KV_HEADS, rep, HEAD_DIM) + shape[axis + 1:])
    return jnp.swapaxes(x, axis, axis + 1).reshape(shape)


def _prep_weights(w_in, w_out, pool_w):
    depth = w_in.shape[0]
    w_in, w_out = w_in.astype(BF16), w_out.astype(BF16)
    o_q, o_k = 2 * CONV_W + POOL_W, 2 * CONV_W + POOL_W + ATTN_W
    o_qi = o_k + 2 * N_KV_HEADS * HEAD_DIM
    o_ki = o_qi + IDX_HEADS * IDX_DIM
    zeros = lambda n: jnp.zeros((depth, D_MODEL, n), w_in.dtype)
    parts = [w_in[..., :o_q], _head_pairs(w_in[..., o_q:o_k], 2), w_in[..., o_k:o_qi]]
    for h in range(IDX_HEADS):
        parts += [w_in[..., o_qi + h * IDX_DIM:o_qi + (h + 1) * IDX_DIM], zeros(LANES - IDX_DIM)]
    parts += [w_in[..., o_ki:], zeros(LANES - IDX_DIM - IDX_HEADS)]
    win = jnp.concatenate(parts, axis=-1)
    wocp = w_out[:, :CONV_W + POOL_W]
    woa = _head_pairs(w_out[:, CONV_W + POOL_W:], 1)
    plw = jnp.zeros((depth, POOL_W, POOL_W), pool_w.dtype)
    for g in range(len(POOL_WINDOWS)):
        plw = plw.at[:, g * POOL_GW:(g + 1) * POOL_GW, g * POOL_GW:(g + 1) * POOL_GW].set(pool_w[:, g])
    return win, wocp, woa, plw.astype(BF16)


def kernel(x_prompt, x_sample, cache_k, cache_v, cache_kidx, state_conv, state_pool, page_table,
           ffn1_norm_pre, ffn1_w_gate, ffn1_w_up, ffn1_w_down, ffn1_norm_post,
           mix_norm_pre, w_in, conv_dw, conv_dw_bias, conv_ln_gain, conv_ln_bias, conv_pw,
           pool_w, pool_scale, w_out, mix_norm_post,
           ffn2_norm_pre, ffn2_w_gate, ffn2_w_up, ffn2_w_down, ffn2_norm_post):
    bp, tp, _ = x_prompt.shape
    bs, ts, _ = x_sample.shape
    assert ts == 1 and tp % Q_BLOCK == 0
    depth = w_in.shape[0]
    past = page_table.shape[1] * PAGE_SIZE

    win, wocp, woa, plw = _prep_weights(w_in, w_out, pool_w)
    vec = lambda a: a.reshape(depth, 1, -1)
    stacks = dict(
        n1=vec(ffn1_norm_pre), wg1=ffn1_w_gate.astype(BF16), wu1=ffn1_w_up.astype(BF16),
        wd1=ffn1_w_down.astype(BF16), n1p=vec(ffn1_norm_post), nm=vec(mix_norm_pre), win=win,
        dw=conv_dw, dwb=vec(conv_dw_bias), lng=vec(conv_ln_gain), lnb=vec(conv_ln_bias),
        pw=conv_pw.astype(BF16), plw=plw, psc=vec(pool_scale), wocp=wocp, woa=woa,
        nmp=vec(mix_norm_post), n2=vec(ffn2_norm_pre), wg2=ffn2_w_gate.astype(BF16),
        wu2=ffn2_w_up.astype(BF16), wd2=ffn2_w_down.astype(BF16), n2p=vec(ffn2_norm_post))

    tm_p = 512 if tp % 512 == 0 else Q_BLOCK
    tt_p = 512 if tp % 512 == 0 else Q_BLOCK
    rope_p = _rope_tables(tp, 0, False)
    rope_s = _rope_tables(bs * ts, past, True)
    topk_p = min(INDEX_TOPK, tp // 4)
    topk_s = min(INDEX_TOPK, (past + ts) // 4)
    zc = jnp.zeros((bp, CONV_K - 1, CONV_W), F32)
    zp = jnp.zeros((bp, POOL_BUF, POOL_W), F32)

    hp = x_prompt.reshape(bp * tp, D_MODEL)
    hs = x_sample.reshape(bs * ts, D_MODEL)
    acc_p, acc_s = [], []
    lw = stacks
    for l in range(depth):
        hp, g, xpool, q, k, v, kb, vb, qi, kiwi, kiwib = _ffn_in(hp, lw, l, rope_p, tm_p, tp // tm_p)
        seq = lambda a: a.reshape(bp, tp, a.shape[-1])
        ycp, nconv, npool = _mix_seq(seq(g), seq(xpool), zc, zp, lw, l, 0, tt_p)
        ya = _dsa_prompt(seq(q), seq(qi), seq(kiwi), seq(kb), seq(vb), seq(kiwib), topk_p)
        hp = _ffn_out(hp, ycp.reshape(bp * tp, -1), ya.reshape(bp * tp, -1), lw, l, tm_p)
        acc_p.append((k.reshape(bp, tp, N_KV_HEADS, HEAD_DIM), v.reshape(bp, tp, N_KV_HEADS, HEAD_DIM),
                      kiwi[:, :IDX_DIM].reshape(bp, tp, IDX_DIM), nconv, npool))
        hs, g, xpool, q, k, v, kb, vb, qi, kiwi, kiwib = _ffn_in(hs, lw, l, rope_s, bs * ts, 1)
        seq = lambda a: a.reshape(bs, ts, a.shape[-1])
        ycp, nconv, npool = _mix_seq(seq(g), seq(xpool), state_conv[l], state_pool[l], lw, l, past, ts)
        ya = _dsa_sample(q, qi, kiwi, kb, vb, kiwib, cache_k, cache_v, cache_kidx, page_table, l, topk_s)
        hs = _ffn_out(hs, ycp.reshape(bs * ts, -1), ya, lw, l, bs * ts)
        acc_s.append((k.reshape(bs, ts, N_KV_HEADS, HEAD_DIM), v.reshape(bs, ts, N_KV_HEADS, HEAD_DIM),
                      kiwi[:, :IDX_DIM].reshape(bs, ts, IDX_DIM), nconv, npool))

    stack = lambda acc, i: jnp.stack([a[i] for a in acc])
    return ((hp.reshape(bp, tp, D_MODEL), hs.reshape(bs, ts, D_MODEL))
            + tuple(stack(acc_p, i) for i in range(5)) + tuple(stack(acc_s, i) for i in range(5)))
```

```python
import functools

import jax
import jax.numpy as jnp
from jax import lax
from jax.experimental import pallas as pl
from jax.experimental.pallas import tpu as pltpu

D_MODEL = 1024
PAGE_SIZE = 128
CONV_W = D_MODEL // 4
CONV_K = 31
POOL_W = D_MODEL // 4
POOL_WINDOWS = (2, 4, 8, 16)
POOL_GW = POOL_W // len(POOL_WINDOWS)
POOL_BUF = max(POOL_WINDOWS) - 1
HEAD_DIM = 64
ATTN_W = D_MODEL - CONV_W - POOL_W
N_HEADS = ATTN_W // HEAD_DIM
N_KV_HEADS = 2
IDX_HEADS = 4
IDX_DIM = 64
INDEX_TOPK = 256
ROPE_THETA = 500000.0
ROPE_ROT = HEAD_DIM // 4
ROPE_HALF = ROPE_ROT // 2
Q_BLOCK = 128
D_FF = 2816
NORM_EPS = 1e-6

Q_SCALE = HEAD_DIM ** -0.5 * 1.4426950408889634

LANES = 128
V7X_VMEM_BYTES = 64 * 1024 * 1024
VMEM_LIMIT = V7X_VMEM_BYTES - 8 * 1024 * 1024
INT32_MIN = -(2 ** 31)

C_CONV_V, C_CONV_G, C_POOL, C_Q, C_K, C_V, C_QI, C_KIWI, C_END = (
    0, 2 * LANES, 4 * LANES, 6 * LANES, 10 * LANES, 11 * LANES, 12 * LANES, 16 * LANES, 17 * LANES)

F32 = jnp.float32
BF16 = jnp.bfloat16


def _layer_spec(layer, shape):
    return pl.BlockSpec((None,) + shape, lambda *_: (layer,) + (0,) * len(shape),
                        pipeline_mode=pl.Buffered(1))


def _rms(x, g):
    ms = jnp.mean(x * x, axis=-1, keepdims=True)
    return x * lax.rsqrt(ms + NORM_EPS) * g


def _swiglu(xn, wg_ref, wu_ref, wd_ref):
    g = jnp.dot(xn, wg_ref[...], preferred_element_type=F32)
    u = jnp.dot(xn, wu_ref[...], preferred_element_type=F32)
    a = (g * jax.nn.sigmoid(g)) * u
    return jnp.dot(a.astype(BF16), wd_ref[...], preferred_element_type=F32)


def _macaron(x, npre_ref, wg_ref, wu_ref, wd_ref, npost_ref):
    y = _swiglu(_rms(x, npre_ref[...]).astype(BF16), wg_ref, wu_ref, wd_ref)
    return x + 0.5 * _rms(y, npost_ref[...])


def _rope_table_kernel(inv_ref, c_ref, sa_ref, sb_ref, *, pos0, same_pos):
    rows = c_ref.shape[0]
    r = lax.broadcasted_iota(jnp.int32, (rows, LANES), 0)
    pos = (jnp.full_like(r, pos0) if same_pos else pos0 + r).astype(F32)
    lane = lax.broadcasted_iota(jnp.int32, (rows, LANES), 1)
    within = lane & (HEAD_DIM - 1)
    ang = pos * inv_ref[...]
    cos, sin = jnp.cos(ang), jnp.sin(ang)
    c_ref[...] = jnp.where(within < ROPE_ROT, cos, 1.0)
    sa_ref[...] = jnp.where(within < ROPE_HALF, -sin, 0.0)
    sb_ref[...] = jnp.where((within >= ROPE_HALF) & (within < ROPE_ROT), sin, 0.0)


def _rope_tables(rows, pos0, same_pos):
    j = jnp.arange(LANES) % ROPE_HALF
    inv = ROPE_THETA ** (-(j.astype(F32)) * 2.0 / ROPE_ROT)
    out = jax.ShapeDtypeStruct((rows, LANES), F32)
    return pl.pallas_call(
        functools.partial(_rope_table_kernel, pos0=pos0, same_pos=same_pos),
        out_shape=(out, out, out), name="rope_tables",
    )(inv.reshape(1, LANES))


def _rope_block(z, c, sa, sb):
    return z * c + pltpu.roll(z, LANES - ROPE_HALF, 1) * sa + pltpu.roll(z, ROPE_HALF, 1) * sb


def _ffn_in_kernel(x_ref, n1_ref, wg_ref, wu_ref, wd_ref, n1p_ref, nm_ref, win_ref,
                   rc_ref, rsa_ref, rsb_ref,
                   h_ref, g_ref, pool_ref, q_ref, k_ref, v_ref, kb_ref, vb_ref,
                   qi_ref, kiwi_ref, kiwib_ref):
    h = _macaron(x_ref[...], n1_ref, wg_ref, wu_ref, wd_ref, n1p_ref)
    h_ref[...] = h
    u = _rms(h, nm_ref[...]).astype(BF16)
    a = jnp.dot(u, win_ref[...], preferred_element_type=F32)
    g_ref[...] = a[:, C_CONV_V:C_CONV_G] * jax.nn.sigmoid(a[:, C_CONV_G:C_POOL])
    pool_ref[...] = a[:, C_POOL:C_Q]

    c, sa, sb = rc_ref[...], rsa_ref[...], rsb_ref[...]
    lane = lax.broadcasted_iota(jnp.int32, c.shape, 1)
    first = lane < HEAD_DIM
    c1, sa1, sb1 = jnp.where(first, c, 1.0), jnp.where(first, sa, 0.0), jnp.where(first, sb, 0.0)

    for i in range(4):
        z = _rope_block(a[:, C_Q + i * LANES:C_Q + (i + 1) * LANES], c, sa, sb)
        q_ref[:, i * LANES:(i + 1) * LANES] = (z * Q_SCALE).astype(BF16)
        z = _rope_block(a[:, C_QI + i * LANES:C_QI + (i + 1) * LANES], c1, sa1, sb1)
        qi_ref[:, i * LANES:(i + 1) * LANES] = z.astype(BF16)
    k = _rope_block(a[:, C_K:C_V], c, sa, sb)
    k_ref[...] = k
    kb_ref[...] = k.astype(BF16)
    v = a[:, C_V:C_QI]
    v_ref[...] = v
    vb_ref[...] = v.astype(BF16)
    kiwi = _rope_block(a[:, C_KIWI:C_END], c1, sa1, sb1)
    kiwi_ref[...] = kiwi
    kiwib_ref[...] = kiwi.astype(BF16)


def _ffn_in(x, lw, layer, rope, tm, rope_blocks):
    n = x.shape[0]
    row = lambda w: pl.BlockSpec((tm, w), lambda i: (i, 0))
    rope_spec = pl.BlockSpec((tm, LANES), lambda i: (i % rope_blocks, 0))
    par = functools.partial(_layer_spec, layer)
    widths = (D_MODEL, CONV_W, POOL_W, 4 * LANES, LANES, LANES, LANES, LANES, 4 * LANES, LANES, LANES)
    dtypes = (F32, F32, F32, BF16, F32, F32, BF16, BF16, BF16, F32, BF16)
    return pl.pallas_call(
        _ffn_in_kernel,
        grid=(n // tm,),
        in_specs=[row(D_MODEL), par((1, D_MODEL)), par((D_MODEL, D_FF)),
                  par((D_MODEL, D_FF)), par((D_FF, D_MODEL)), par((1, D_MODEL)),
                  par((1, D_MODEL)), par((D_MODEL, C_END)),
                  rope_spec, rope_spec, rope_spec],
        out_specs=[row(w) for w in widths],
        out_shape=[jax.ShapeDtypeStruct((n, w), d) for w, d in zip(widths, dtypes)],
        compiler_params=pltpu.CompilerParams(dimension_semantics=("arbitrary",),
                                             vmem_limit_bytes=VMEM_LIMIT),
        name="ffn_in",
    )(x, lw["n1"], lw["wg1"], lw["wu1"], lw["wd1"], lw["n1p"], lw["nm"], lw["win"], *rope)


def _ffn_out_kernel(h_ref, ycp_ref, ya_ref, wocp_ref, woa_ref, nmp_ref,
                    n2_ref, wg_ref, wu_ref, wd_ref, n2p_ref, o_ref):
    y = (jnp.dot(ycp_ref[...], wocp_ref[...], preferred_element_type=F32)
         + jnp.dot(ya_ref[...], woa_ref[...], preferred_element_type=F32))
    h = h_ref[...] + _rms(y, nmp_ref[...])
    o_ref[...] = _macaron(h, n2_ref, wg_ref, wu_ref, wd_ref, n2p_ref)


def _ffn_out(h, ycp, ya, lw, layer, tm):
    n = h.shape[0]
    row = lambda w: pl.BlockSpec((tm, w), lambda i: (i, 0))
    par = functools.partial(_layer_spec, layer)
    return pl.pallas_call(
        _ffn_out_kernel,
        grid=(n // tm,),
        in_specs=[row(D_MODEL), row(CONV_W + POOL_W), row(ATTN_W),
                  par((CONV_W + POOL_W, D_MODEL)), par((ATTN_W, D_MODEL)),
                  par((1, D_MODEL)), par((1, D_MODEL)),
                  par((D_MODEL, D_FF)), par((D_MODEL, D_FF)),
                  par((D_FF, D_MODEL)), par((1, D_MODEL))],
        out_specs=row(D_MODEL),
        out_shape=jax.ShapeDtypeStruct((n, D_MODEL), F32),
        compiler_params=pltpu.CompilerParams(dimension_semantics=("arbitrary",),
                                             vmem_limit_bytes=VMEM_LIMIT),
        name="ffn_out",
    )(h, ycp, ya, lw["wocp"], lw["woa"], lw["nmp"], lw["n2"], lw["wg2"], lw["wu2"], lw["wd2"], lw["n2p"])


CONV_HIST = 32
POOL_HIST = 16


def _mix_seq_kernel(g_ref, x_ref, pc_ref, pp_ref, dw_ref, dwb_ref, lng_ref, lnb_ref, pw_ref,
                    plw_ref, psc_ref, y_ref, nc_ref, np_ref, gbuf, pbuf, sbuf, *, tt, ttp, pos0, n_t):
    t = pl.program_id(1)
    c0, p0 = CONV_HIST - (CONV_K - 1), POOL_HIST - POOL_BUF

    @pl.when(t == 0)
    def _():
        gbuf[c0:CONV_HIST, :] = pc_ref[0]
        pbuf[p0:POOL_HIST, :] = pp_ref[0]

    if ttp != tt:
        gbuf[CONV_HIST:CONV_HIST + ttp, :] = jnp.zeros((ttp, CONV_W), F32)
        pbuf[POOL_HIST:POOL_HIST + ttp, :] = jnp.zeros((ttp, POOL_W), F32)
    gbuf[CONV_HIST:CONV_HIST + tt, :] = g_ref[0]
    pbuf[POOL_HIST:POOL_HIST + tt, :] = x_ref[0]

    acc = jnp.zeros((ttp, CONV_W), F32) + dwb_ref[...]
    for b in range(8):
        taps = range(b, CONV_K, 8)
        span = ttp + taps[-1] - b
        sbuf[0:span, :] = gbuf[c0 + b:c0 + b + span, :]
        for j in taps:
            acc = acc + sbuf[j - b:j - b + ttp, :] * dw_ref[j:j + 1, :]
    mu = jnp.mean(acc, axis=-1, keepdims=True)
    d = acc - mu
    var = jnp.mean(d * d, axis=-1, keepdims=True)
    yn = d * lax.rsqrt(var + NORM_EPS) * lng_ref[...] + lnb_ref[...]
    yn = yn * jax.nn.sigmoid(yn)
    yc = jnp.dot(yn.astype(BF16), pw_ref[...], preferred_element_type=F32)

    def shifted(j, blk):
        return pbuf[POOL_HIST - j:POOL_HIST - j + ttp, blk * LANES:(blk + 1) * LANES]

    pos = pos0 + t * tt + lax.broadcasted_iota(jnp.int32, (ttp, LANES), 0)
    lane = lax.broadcasted_iota(jnp.int32, (ttp, LANES), 1)
    lo = lane < POOL_GW
    sums = []
    for blk, (w_lo, w_hi) in enumerate(((POOL_WINDOWS[0], POOL_WINDOWS[1]),
                                        (POOL_WINDOWS[2], POOL_WINDOWS[3]))):
        s = shifted(0, blk)
        for j in range(1, w_lo):
            s = s + shifted(j, blk)
        s_lo = s
        for j in range(w_lo, w_hi):
            s = s + shifted(j, blk)
        cnt = jnp.minimum(jnp.where(lo, w_lo, w_hi), pos + 1).astype(F32)
        sums.append(jnp.where(lo, s_lo, s) / cnt - shifted(0, blk))
    pooled = jnp.concatenate(sums, axis=1)
    yp = jnp.dot(pooled.astype(BF16), plw_ref[...], preferred_element_type=F32) * psc_ref[...]

    y_ref[0, :, 0:CONV_W] = yc[0:tt].astype(BF16)
    y_ref[0, :, CONV_W:CONV_W + POOL_W] = yp[0:tt].astype(BF16)

    @pl.when(t == n_t - 1)
    def _():
        nc_ref[0] = gbuf[c0 + tt:CONV_HIST + tt, :]
        np_ref[0] = pbuf[p0 + tt:POOL_HIST + tt, :]

    if n_t > 1:
        @pl.when(t < n_t - 1)
        def _():
            gbuf[0:CONV_HIST, :] = gbuf[tt:tt + CONV_HIST, :]
            pbuf[0:POOL_HIST, :] = pbuf[tt:tt + POOL_HIST, :]


def _mix_seq(g, x, prev_conv, prev_pool, lw, layer, pos0, tt):
    b, t_len, _ = g.shape
    n_t = t_len // tt
    assert n_t * tt == t_len and (n_t == 1 or tt % 8 == 0 and tt >= CONV_HIST)
    ttp = -(-tt // 8) * 8
    tile = lambda w: pl.BlockSpec((1, tt, w), lambda i, j: (i, j, 0))
    state = lambda r, w: pl.BlockSpec((1, r, w), lambda i, j: (i, 0, 0))
    par = functools.partial(_layer_spec, layer)
    return pl.pallas_call(
        functools.partial(_mix_seq_kernel, tt=tt, ttp=ttp, pos0=pos0, n_t=n_t),
        grid=(b, n_t),
        in_specs=[tile(CONV_W), tile(POOL_W), state(CONV_K - 1, CONV_W), state(POOL_BUF, POOL_W),
                  par((CONV_K, CONV_W)), par((1, CONV_W)), par((1, CONV_W)),
                  par((1, CONV_W)), par((CONV_W, CONV_W)),
                  par((POOL_W, POOL_W)), par((1, POOL_W))],
        out_specs=[tile(CONV_W + POOL_W), state(CONV_K - 1, CONV_W), state(POOL_BUF, POOL_W)],
        out_shape=[jax.ShapeDtypeStruct((b, t_len, CONV_W + POOL_W), BF16),
                   jax.ShapeDtypeStruct((b, CONV_K - 1, CONV_W), F32),
                   jax.ShapeDtypeStruct((b, POOL_BUF, POOL_W), F32)],
        scratch_shapes=[pltpu.VMEM((CONV_HIST + ttp, CONV_W), F32),
                        pltpu.VMEM((POOL_HIST + ttp, POOL_W), F32),
                        pltpu.VMEM((CONV_HIST + ttp, CONV_W), F32)],
        compiler_params=pltpu.CompilerParams(dimension_semantics=("arbitrary", "arbitrary"),
                                             vmem_limit_bytes=VMEM_LIMIT),
        name="mix_seq",
    )(g, x, prev_conv, prev_pool, lw["dw"], lw["dwb"], lw["lng"], lw["lnb"], lw["pw"],
      lw["plw"], lw["psc"])


COUNT_GROUPS = 8


def _column_count(mask, negate=False):
    n = mask.shape[0] // 8
    mask = mask.reshape(n, 8, LANES)
    accs = [jnp.zeros((8, LANES), F32) for _ in range(min(COUNT_GROUPS, n))]
    for i in range(n):
        a = accs[i % len(accs)]
        accs[i % len(accs)] = jnp.where(mask[i], a, a + 1.0) if negate else jnp.where(mask[i], a + 1.0, a)
    while len(accs) > 1:
        accs = [a + b for a, b in zip(accs[0::2], accs[1::2])] + accs[len(accs) & ~1:]
    return jnp.sum(accs[0], axis=0, keepdims=True)


def _ordered_bits_to_float(key):
    return pltpu.bitcast(key ^ ((key >> 31) & 0x7FFFFFFF), F32)


SELECT_STEPS = 32


def _select_bias_t(score_t, valid_t, topk, sc_ref):
    w = score_t.shape[0]
    sc_ref[...] = jnp.where(valid_t, score_t, -jnp.inf)

    def bisect(it, key):
        cand = key + jnp.left_shift(jnp.int32(1), SELECT_STEPS - 1 - it)
        cnt = _column_count(sc_ref[...] < _ordered_bits_to_float(cand), negate=True)
        return jnp.where(cnt >= topk, cand, key)

    key = lax.fori_loop(0, SELECT_STEPS, bisect, jnp.full((1, LANES), INT32_MIN, jnp.int32))
    thr = _ordered_bits_to_float(key)
    sc = sc_ref[...]
    gt = sc > thr
    need = topk - _column_count(gt)
    eq = sc == thr
    eqf = jnp.where(eq, 1.0, 0.0).astype(BF16)
    r = lax.broadcasted_iota(jnp.int32, (2 * LANES, LANES), 0)
    c = lax.broadcasted_iota(jnp.int32, (2 * LANES, LANES), 1)
    tri = jnp.where((r >= LANES) | (c < r), 1.0, 0.0).astype(BF16)
    offset = jnp.zeros((1, LANES), F32)
    ranks = []
    n_ch = w // LANES
    for ch0 in range(0, n_ch, 2):
        chunks = range(ch0, min(ch0 + 2, n_ch))
        rhs = jnp.concatenate([eqf[ch * LANES:(ch + 1) * LANES, :] for ch in chunks], axis=1)
        res2 = jnp.dot(tri, rhs, preferred_element_type=F32)
        for ch in chunks:
            res = res2[:, (ch - ch0) * LANES:(ch - ch0 + 1) * LANES]
            ranks.append(res[0:LANES] + offset)
            offset = offset + res[LANES:LANES + 1]
    rank = jnp.concatenate(ranks, axis=0)
    sel = valid_t & (gt | (eq & (rank < need)))
    return jnp.where(sel, 0.0, -jnp.inf)


def _dsa_prompt_kernel(q_ref, qi_ref, wi_ref, k_ref, v_ref, ki_ref, o_ref, sc_ref, *, w, blk0, topk):
    t0 = (blk0 + pl.program_id(1)) * Q_BLOCK
    ki = ki_ref[0]
    wi_t = jnp.transpose(wi_ref[0])
    scale = IDX_DIM ** -0.5 * IDX_HEADS ** -0.5
    score_t = jnp.zeros((w, Q_BLOCK), F32)
    for h0 in range(0, IDX_HEADS, 2):
        qh = jnp.concatenate([qi_ref[0, :, h * LANES:(h + 1) * LANES] for h in (h0, h0 + 1)], axis=0)
        d2 = lax.dot_general(ki, qh, (((1,), (1,)), ((), ())), preferred_element_type=F32)
        for h in (h0, h0 + 1):
            d = d2[:, (h - h0) * Q_BLOCK:(h - h0 + 1) * Q_BLOCK]
            score_t = score_t + jnp.maximum(d, 0.0) * (wi_t[IDX_DIM + h:IDX_DIM + h + 1, :] * scale)
    s_idx = lax.broadcasted_iota(jnp.int32, (w, Q_BLOCK), 0)
    t_idx = t0 + lax.broadcasted_iota(jnp.int32, (w, Q_BLOCK), 1)
    bias = jnp.transpose(_select_bias_t(score_t, s_idx <= t_idx, topk, sc_ref))

    k = k_ref[0]
    v = v_ref[0]
    lane = lax.broadcasted_iota(jnp.int32, (Q_BLOCK, LANES), 1)
    lo = lane < HEAD_DIM
    rep = N_HEADS // N_KV_HEADS
    outs = []
    for g in range(N_KV_HEADS):
        qs = [jnp.where(lo if g == 0 else ~lo, q_ref[0, :, i * LANES:(i + 1) * LANES], 0)
              for i in range(rep)]
        qg = jnp.concatenate(qs, axis=0)
        s = lax.dot_general(qg, k, (((1,), (1,)), ((), ())), preferred_element_type=F32)
        s = s.reshape(rep, Q_BLOCK, w) + bias[None]
        m = jnp.max(s, axis=-1, keepdims=True)
        p = jnp.exp2(s - m)
        l = jnp.sum(p, axis=-1, keepdims=True)
        o = jnp.dot(p.reshape(rep * Q_BLOCK, w).astype(BF16), v, preferred_element_type=F32)
        outs.append(o.reshape(rep, Q_BLOCK, LANES) / l)
    for i in range(rep):
        o_ref[0, :, i * LANES:(i + 1) * LANES] = jnp.where(lo, outs[0][i], outs[1][i]).astype(BF16)


def _dsa_prompt(q, qi, kiwi, kb, vb, kiwib, topk):
    b, t_len, _ = q.shape
    nblk = t_len // Q_BLOCK
    step = min(2, nblk)
    outs = []
    for blk0 in range(0, nblk, step):
        nb = min(step, nblk - blk0)
        w = (blk0 + nb) * Q_BLOCK
        qspec = lambda wd: pl.BlockSpec((1, Q_BLOCK, wd), lambda i, j, blk0=blk0: (i, blk0 + j, 0))
        kspec = pl.BlockSpec((1, w, LANES), lambda i, j: (i, 0, 0))
        outs.append(pl.pallas_call(
            functools.partial(_dsa_prompt_kernel, w=w, blk0=blk0, topk=topk),
            grid=(b, nb),
            in_specs=[qspec(ATTN_W), qspec(IDX_HEADS * LANES), qspec(LANES), kspec, kspec, kspec],
            out_specs=pl.BlockSpec((1, Q_BLOCK, ATTN_W), lambda i, j: (i, j, 0)),
            out_shape=jax.ShapeDtypeStruct((b, nb * Q_BLOCK, ATTN_W), BF16),
            scratch_shapes=[pltpu.VMEM((w, Q_BLOCK), F32)],
            compiler_params=pltpu.CompilerParams(dimension_semantics=("arbitrary", "arbitrary"),
                                                 vmem_limit_bytes=VMEM_LIMIT),
            name=f"dsa_prompt_w{w}",
        )(q, qi, kiwi, kb, vb, kiwib))
    return jnp.concatenate(outs, axis=1) if len(outs) > 1 else outs[0]


def _fetch_pages(pt_ref, cache_ref, layer, buf, sem, seq, slot, n_pages, start):
    for j in range(n_pages):
        page = pt_ref[seq, j] if start else 0
        cp = pltpu.make_async_copy(cache_ref.at[layer, page],
                                   buf.at[slot, :, pl.ds(j * PAGE_SIZE, PAGE_SIZE)], sem.at[slot])
        if start:
            cp.start()
        else:
            cp.wait()


def _pipelined_pages(pt_ref, caches, layer, bufs, sems, n_pages):
    b, nb = pl.program_id(0), pl.num_programs(0)

    @pl.when(b == 0)
    def _():
        for cache, buf, sem in zip(caches, bufs, sems):
            _fetch_pages(pt_ref, cache, layer, buf, sem, 0, 0, n_pages, True)

    @pl.when(b + 1 < nb)
    def _():
        for cache, buf, sem in zip(caches, bufs, sems):
            _fetch_pages(pt_ref, cache, layer, buf, sem, b + 1, (b + 1) % 2, n_pages, True)

    for cache, buf, sem in zip(caches, bufs, sems):
        _fetch_pages(pt_ref, cache, layer, buf, sem, b, b % 2, n_pages, False)
    return b % 2


def _sample_index_kernel(pt_ref, qi_ref, w_ref, kinew_ref, cache_ref, o_ref, buf, sem, *, layer, n_pages):
    slot = _pipelined_pages(pt_ref, (cache_ref,), layer, (buf,), (sem,), n_pages)
    past = n_pages * PAGE_SIZE
    qi = qi_ref[0]
    wcol = w_ref[0][:, 0:1] * (IDX_DIM ** -0.5 * IDX_HEADS ** -0.5)
    kb = buf[slot].astype(BF16)
    d = jnp.dot(qi[:, 0:IDX_DIM], kb, preferred_element_type=F32)
    o_ref[0, :, 0:past] = jnp.sum(jnp.maximum(d, 0.0) * wcol, axis=0, keepdims=True)
    d_new = jnp.sum(qi.astype(F32) * kinew_ref[0].astype(F32), axis=1, keepdims=True)
    s_new = jnp.sum(jnp.maximum(d_new, 0.0) * wcol, axis=0, keepdims=True)
    lane = lax.broadcasted_iota(jnp.int32, (1, LANES), 1)
    o_ref[0, :, past:past + LANES] = jnp.where(lane == 0, s_new, -jnp.inf)


def _sample_select_kernel(s_ref, o_ref, sc_ref, *, n_valid, topk):
    bs, w = s_ref.shape
    pad = jnp.full((LANES - bs, w), -jnp.inf, F32)
    score_t = jnp.transpose(jnp.concatenate([s_ref[...], pad], axis=0))
    row = lax.broadcasted_iota(jnp.int32, (w, LANES), 0)
    bias_t = _select_bias_t(score_t, row < n_valid, topk, sc_ref)
    o_ref[...] = jnp.transpose(bias_t)[0:bs]


def _sample_attend_kernel(pt_ref, q_ref, bias_ref, knew_ref, vnew_ref, ck_ref, cv_ref, o_ref,
                          kbuf, vbuf, ksem, vsem, *, layer, n_pages):
    slot = _pipelined_pages(pt_ref, (ck_ref, cv_ref), layer, (kbuf, vbuf), (ksem, vsem), n_pages)
    past = n_pages * PAGE_SIZE
    q = q_ref[0]
    s = jnp.dot(q, kbuf[slot].astype(BF16), preferred_element_type=F32)
    s = s + bias_ref[0, :, 0:past]
    s_new = (jnp.sum(q.astype(F32) * knew_ref[0].astype(F32), axis=1, keepdims=True)
             + bias_ref[0, :, past:past + 1])
    m = jnp.maximum(jnp.max(s, axis=1, keepdims=True), s_new)
    p = jnp.exp2(s - m)
    p_new = jnp.exp2(s_new - m)
    l = jnp.sum(p, axis=1, keepdims=True) + p_new
    o = lax.dot_general(p.astype(BF16), vbuf[slot].astype(BF16), (((1,), (1,)), ((), ())),
                        preferred_element_type=F32)
    o = (o + p_new.astype(BF16).astype(F32) * vnew_ref[0].astype(F32)) / l
    lane = lax.broadcasted_iota(jnp.int32, (1, LANES), 1)
    rep = N_HEADS // N_KV_HEADS
    for i in range(rep):
        o_ref[0, :, i * LANES:(i + 1) * LANES] = jnp.where(
            lane < HEAD_DIM, o[i:i + 1], o[i + rep:i + rep + 1]).astype(BF16)


def _dsa_sample(q, qi, kiwi, kb, vb, kiwib, cache_k, cache_v, cache_kidx, page_table, layer, topk):
    bs = q.shape[0]
    n_pages = page_table.shape[1]
    past = n_pages * PAGE_SIZE
    wp = past + LANES
    rows = 16
    rep = N_HEADS // N_KV_HEADS
    lane = jnp.arange(LANES)

    qi16 = jnp.pad(qi.reshape(bs, IDX_HEADS, LANES), ((0, 0), (0, rows - IDX_HEADS), (0, 0)))
    w16 = jnp.pad(jnp.broadcast_to(kiwi[:, IDX_DIM:IDX_DIM + IDX_HEADS, None], (bs, IDX_HEADS, LANES)),
                  ((0, 0), (0, rows - IDX_HEADS), (0, 0)))
    seq = lambda r, wd: pl.BlockSpec((1, r, wd), lambda i, pt: (i, 0, 0))
    any_spec = pl.BlockSpec(memory_space=pl.ANY)
    n_phys = cache_k.shape[1]
    ckidx_t = jnp.swapaxes(cache_kidx, 2, 3)
    ck_t = cache_k.transpose(0, 1, 3, 4, 2).reshape(cache_k.shape[0], n_phys, LANES, PAGE_SIZE)
    cv_t = cache_v.transpose(0, 1, 3, 4, 2).reshape(cache_v.shape[0], n_phys, LANES, PAGE_SIZE)
    scores = pl.pallas_call(
        functools.partial(_sample_index_kernel, layer=layer, n_pages=n_pages),
        grid_spec=pltpu.PrefetchScalarGridSpec(
            num_scalar_prefetch=1, grid=(bs,),
            in_specs=[seq(rows, LANES), seq(rows, LANES), seq(1, LANES), any_spec],
            out_specs=seq(1, wp),
            scratch_shapes=[pltpu.VMEM((2, IDX_DIM, past), F32), pltpu.SemaphoreType.DMA((2,))]),
        out_shape=jax.ShapeDtypeStruct((bs, 1, wp), F32),
        compiler_params=pltpu.CompilerParams(dimension_semantics=("arbitrary",),
                                             vmem_limit_bytes=VMEM_LIMIT),
        name="sample_index",
    )(page_table, qi16, w16, kiwib.reshape(bs, 1, LANES), ckidx_t)

    bias = pl.pallas_call(
        functools.partial(_sample_select_kernel, n_valid=past + 1, topk=topk),
        out_shape=jax.ShapeDtypeStruct((bs, wp), F32),
        scratch_shapes=[pltpu.VMEM((wp, LANES), F32)],
        compiler_params=pltpu.CompilerParams(vmem_limit_bytes=VMEM_LIMIT),
        name="sample_select",
    )(scores.reshape(bs, wp)).reshape(bs, 1, wp)

    q4 = q.reshape(bs, rep, LANES)
    q16 = jnp.concatenate([jnp.where(lane < HEAD_DIM, q4, 0), jnp.where(lane >= HEAD_DIM, q4, 0),
                           jnp.zeros((bs, rows - N_HEADS, LANES), q.dtype)], axis=1)
    return pl.pallas_call(
        functools.partial(_sample_attend_kernel, layer=layer, n_pages=n_pages),
        grid_spec=pltpu.PrefetchScalarGridSpec(
            num_scalar_prefetch=1, grid=(bs,),
            in_specs=[seq(rows, LANES), seq(1, wp), seq(1, LANES), seq(1, LANES), any_spec, any_spec],
            out_specs=seq(1, ATTN_W),
            scratch_shapes=[pltpu.VMEM((2, LANES, past), F32), pltpu.VMEM((2, LANES, past), F32),
                            pltpu.SemaphoreType.DMA((2,)), pltpu.SemaphoreType.DMA((2,))]),
        out_shape=jax.ShapeDtypeStruct((bs, 1, ATTN_W), BF16),
        compiler_params=pltpu.CompilerParams(dimension_semantics=("arbitrary",),
                                             vmem_limit_bytes=VMEM_LIMIT),
        name="sample_attend",
    )(page_table, q16, bias, kb.reshape(bs, 1, LANES), vb.reshape(bs, 1, LANES), ck_t, cv_t).reshape(bs, ATTN_W)


def _head_pairs(x, axis):
    rep = N_HEADS // N_KV_HEADS
    shape = x.shape
    x = x.reshape(shape[:axis] + (N_KV_HEADS, rep, HEAD_DIM) + shape[axis + 1:])
    return jnp.swapaxes(x, axis, axis + 1).reshape(shape)


def _prep_weights(w_in, w_out, pool_w):
    depth = w_in.shape[0]
    w_in, w_out = w_in.astype(BF16), w_out.astype(BF16)
    o_q, o_k = 2 * CONV_W + POOL_W, 2 * CONV_W + POOL_W + ATTN_W
    o_qi = o_k + 2 * N_KV_HEADS * HEAD_DIM
    o_ki = o_qi + IDX_HEADS * IDX_DIM
    zeros = lambda n: jnp.zeros((depth, D_MODEL, n), w_in.dtype)
    parts = [w_in[..., :o_q], _head_pairs(w_in[..., o_q:o_k], 2), w_in[..., o_k:o_qi]]
    for h in range(IDX_HEADS):
        parts += [w_in[..., o_qi + h * IDX_DIM:o_qi + (h + 1) * IDX_DIM], zeros(LANES - IDX_DIM)]
    parts += [w_in[..., o_ki:], zeros(LANES - IDX_DIM - IDX_HEADS)]
    win = jnp.concatenate(parts, axis=-1)
    wocp = w_out[:, :CONV_W + POOL_W]
    woa = _head_pairs(w_out[:, CONV_W + POOL_W:], 1)
    plw = jnp.zeros((depth, POOL_W, POOL_W), pool_w.dtype)
    for g in range(len(POOL_WINDOWS)):
        plw = plw.at[:, g * POOL_GW:(g + 1) * POOL_GW, g * POOL_GW:(g + 1) * POOL_GW].set(pool_w[:, g])
    return win, wocp, woa, plw.astype(BF16)


def kernel(x_prompt, x_sample, cache_k, cache_v, cache_kidx, state_conv, state_pool, page_table,
           ffn1_norm_pre, ffn1_w_gate, ffn1_w_up, ffn1_w_down, ffn1_norm_post,
           mix_norm_pre, w_in, conv_dw, conv_dw_bias, conv_ln_gain, conv_ln_bias, conv_pw,
           pool_w, pool_scale, w_out, mix_norm_post,
           ffn2_norm_pre, ffn2_w_gate, ffn2_w_up, ffn2_w_down, ffn2_norm_post):
    bp, tp, _ = x_prompt.shape
    bs, ts, _ = x_sample.shape
    assert ts == 1 and tp % Q_BLOCK == 0
    depth = w_in.shape[0]
    past = page_table.shape[1] * PAGE_SIZE

    win, wocp, woa, plw = _prep_weights(w_in, w_out, pool_w)
    vec = lambda a: a.reshape(depth, 1, -1)
    stacks = dict(
        n1=vec(ffn1_norm_pre), wg1=ffn1_w_gate.astype(BF16), wu1=ffn1_w_up.astype(BF16),
        wd1=ffn1_w_down.astype(BF16), n1p=vec(ffn1_norm_post), nm=vec(mix_norm_pre), win=win,
        dw=conv_dw, dwb=vec(conv_dw_bias), lng=vec(conv_ln_gain), lnb=vec(conv_ln_bias),
        pw=conv_pw.astype(BF16), plw=plw, psc=vec(pool_scale), wocp=wocp, woa=woa,
        nmp=vec(mix_norm_post), n2=vec(ffn2_norm_pre), wg2=ffn2_w_gate.astype(BF16),
        wu2=ffn2_w_up.astype(BF16), wd2=ffn2_w_down.astype(BF16), n2p=vec(ffn2_norm_post))

    tm_p = 512 if tp % 512 == 0 else Q_BLOCK
    tt_p = 512 if tp % 512 == 0 else Q_BLOCK
    rope_p = _rope_tables(tp, 0, False)
    rope_s = _rope_tables(bs * ts, past, True)
    topk_p = min(INDEX_TOPK, tp // 4)
    topk_s = min(INDEX_TOPK, (past + ts) // 4)
    zc = jnp.zeros((bp, CONV_K - 1, CONV_W), F32)
    zp = jnp.zeros((bp, POOL_BUF, POOL_W), F32)

    hp = x_prompt.reshape(bp * tp, D_MODEL)
    hs = x_sample.reshape(bs * ts, D_MODEL)
    acc_p, acc_s = [], []
    lw = stacks
    for l in range(depth):
        hp, g, xpool, q, k, v, kb, vb, qi, kiwi, kiwib = _ffn_in(hp, lw, l, rope_p, tm_p, tp // tm_p)
        seq = lambda a: a.reshape(bp, tp, a.shape[-1])
        ycp, nconv, npool = _mix_seq(seq(g), seq(xpool), zc, zp, lw, l, 0, tt_p)
        ya = _dsa_prompt(seq(q), seq(qi), seq(kiwi), seq(kb), seq(vb), seq(kiwib), topk_p)
        hp = _ffn_out(hp, ycp.reshape(bp * tp, -1), ya.reshape(bp * tp, -1), lw, l, tm_p)
        acc_p.append((k.reshape(bp, tp, N_KV_HEADS, HEAD_DIM), v.reshape(bp, tp, N_KV_HEADS, HEAD_DIM),
                      kiwi[:, :IDX_DIM].reshape(bp, tp, IDX_DIM), nconv, npool))
        hs, g, xpool, q, k, v, kb, vb, qi, kiwi, kiwib = _ffn_in(hs, lw, l, rope_s, bs * ts, 1)
        seq = lambda a: a.reshape(bs, ts, a.shape[-1])
        ycp, nconv, npool = _mix_seq(seq(g), seq(xpool), state_conv[l], state_pool[l], lw, l, past, ts)
        ya = _dsa_sample(q, qi, kiwi, kb, vb, kiwib, cache_k, cache_v, cache_kidx, page_table, l, topk_s)
        hs = _ffn_out(hs, ycp.reshape(bs * ts, -1), ya, lw, l, bs * ts)
        acc_s.append((k.reshape(bs, ts, N_KV_HEADS, HEAD_DIM), v.reshape(bs, ts, N_KV_HEADS, HEAD_DIM),
                      kiwi[:, :IDX_DIM].reshape(bs, ts, IDX_DIM), nconv, npool))

    stack = lambda acc, i: jnp.stack([a[i] for a in acc])
    return ((hp.reshape(bp, tp, D_MODEL), hs.reshape(bs, ts, D_MODEL))
            + tuple(stack(acc_p, i) for i in range(5)) + tuple(stack(acc_s, i) for i in range(5)))
```

```python
import functools

import jax
import jax.numpy as jnp
from jax import lax
from jax.experimental import pallas as pl
from jax.experimental.pallas import tpu as pltpu

D_MODEL = 1024
PAGE_SIZE = 128
CONV_W = D_MODEL // 4
CONV_K = 31
POOL_W = D_MODEL // 4
POOL_WINDOWS = (2, 4, 8, 16)
POOL_GW = POOL_W // len(POOL_WINDOWS)
POOL_BUF = max(POOL_WINDOWS) - 1
HEAD_DIM = 64
ATTN_W = D_MODEL - CONV_W - POOL_W
N_HEADS = ATTN_W // HEAD_DIM
N_KV_HEADS = 2
IDX_HEADS = 4
IDX_DIM = 64
INDEX_TOPK = 256
ROPE_THETA = 500000.0
ROPE_ROT = HEAD_DIM // 4
ROPE_HALF = ROPE_ROT // 2
Q_BLOCK = 128
D_FF = 2816
NORM_EPS = 1e-6

Q_SCALE = HEAD_DIM ** -0.5 * 1.4426950408889634

LANES = 128
V7X_VMEM_BYTES = 64 * 1024 * 1024
VMEM_LIMIT = V7X_VMEM_BYTES - 8 * 1024 * 1024
INT32_MIN = -(2 ** 31)

C_CONV_V, C_CONV_G, C_POOL, C_Q, C_K, C_V, C_QI, C_KIWI, C_END = (
    0, 2 * LANES, 4 * LANES, 6 * LANES, 10 * LANES, 11 * LANES, 12 * LANES, 16 * LANES, 17 * LANES)

F32 = jnp.float32
BF16 = jnp.bfloat16


def _layer_spec(layer, shape):
    return pl.BlockSpec((None,) + shape, lambda *_: (layer,) + (0,) * len(shape),
                        pipeline_mode=pl.Buffered(1))


def _rms(x, g):
    ms = jnp.mean(x * x, axis=-1, keepdims=True)
    return x * lax.rsqrt(ms + NORM_EPS) * g


def _swiglu(xn, wg_ref, wu_ref, wd_ref):
    g = jnp.dot(xn, wg_ref[...], preferred_element_type=F32)
    u = jnp.dot(xn, wu_ref[...], preferred_element_type=F32)
    a = (g * jax.nn.sigmoid(g)) * u
    return jnp.dot(a.astype(BF16), wd_ref[...], preferred_element_type=F32)


def _macaron(x, npre_ref, wg_ref, wu_ref, wd_ref, npost_ref):
    y = _swiglu(_rms(x, npre_ref[...]).astype(BF16), wg_ref, wu_ref, wd_ref)
    return x + 0.5 * _rms(y, npost_ref[...])


def _rope_table_kernel(inv_ref, c_ref, sa_ref, sb_ref, *, pos0, same_pos):
    rows = c_ref.shape[0]
    r = lax.broadcasted_iota(jnp.int32, (rows, LANES), 0)
    pos = (jnp.full_like(r, pos0) if same_pos else pos0 + r).astype(F32)
    lane = lax.broadcasted_iota(jnp.int32, (rows, LANES), 1)
    within = lane & (HEAD_DIM - 1)
    ang = pos * inv_ref[...]
    cos, sin = jnp.cos(ang), jnp.sin(ang)
    c_ref[...] = jnp.where(within < ROPE_ROT, cos, 1.0)
    sa_ref[...] = jnp.where(within < ROPE_HALF, -sin, 0.0)
    sb_ref[...] = jnp.where((within >= ROPE_HALF) & (within < ROPE_ROT), sin, 0.0)


def _rope_tables(rows, pos0, same_pos):
    j = jnp.arange(LANES) % ROPE_HALF
    inv = ROPE_THETA ** (-(j.astype(F32)) * 2.0 / ROPE_ROT)
    out = jax.ShapeDtypeStruct((rows, LANES), F32)
    return pl.pallas_call(
        functools.partial(_rope_table_kernel, pos0=pos0, same_pos=same_pos),
        out_shape=(out, out, out), name="rope_tables",
    )(inv.reshape(1, LANES))


def _rope_block(z, c, sa, sb):
    return z * c + pltpu.roll(z, LANES - ROPE_HALF, 1) * sa + pltpu.roll(z, ROPE_HALF, 1) * sb


def _ffn_in_kernel(x_ref, n1_ref, wg_ref, wu_ref, wd_ref, n1p_ref, nm_ref, win_ref,
                   rc_ref, rsa_ref, rsb_ref,
                   h_ref, g_ref, pool_ref, q_ref, k_ref, v_ref, kb_ref, vb_ref,
                   qi_ref, kiwi_ref, kiwib_ref):
    h = _macaron(x_ref[...], n1_ref, wg_ref, wu_ref, wd_ref, n1p_ref)
    h_ref[...] = h
    u = _rms(h, nm_ref[...]).astype(BF16)
    a = jnp.dot(u, win_ref[...], preferred_element_type=F32)
    g_ref[...] = a[:, C_CONV_V:C_CONV_G] * jax.nn.sigmoid(a[:, C_CONV_G:C_POOL])
    pool_ref[...] = a[:, C_POOL:C_Q]

    c, sa, sb = rc_ref[...], rsa_ref[...], rsb_ref[...]
    lane = lax.broadcasted_iota(jnp.int32, c.shape, 1)
    first = lane < HEAD_DIM
    c1, sa1, sb1 = jnp.where(first, c, 1.0), jnp.where(first, sa, 0.0), jnp.where(first, sb, 0.0)

    for i in range(4):
        z = _rope_block(a[:, C_Q + i * LANES:C_Q + (i + 1) * LANES], c, sa, sb)
        q_ref[:, i * LANES:(i + 1) * LANES] = (z * Q_SCALE).astype(BF16)
        z = _rope_block(a[:, C_QI + i * LANES:C_QI + (i + 1) * LANES], c1, sa1, sb1)
        qi_ref[:, i * LANES:(i + 1) * LANES] = z.astype(BF16)
    k = _rope_block(a[:, C_K:C_V], c, sa, sb)
    k_ref[...] = k
    kb_ref[...] = k.astype(BF16)
    v = a[:, C_V:C_QI]
    v_ref[...] = v
    vb_ref[...] = v.astype(BF16)
    kiwi = _rope_block(a[:, C_KIWI:C_END], c1, sa1, sb1)
    kiwi_ref[...] = kiwi
    kiwib_ref[...] = kiwi.astype(BF16)


def _ffn_in(x, lw, layer, rope, tm, rope_blocks):
    n = x.shape[0]
    row = lambda w: pl.BlockSpec((tm, w), lambda i: (i, 0))
    rope_spec = pl.BlockSpec((tm, LANES), lambda i: (i % rope_blocks, 0))
    par = functools.partial(_layer_spec, layer)
    widths = (D_MODEL, CONV_W, POOL_W, 4 * LANES, LANES, LANES, LANES, LANES, 4 * LANES, LANES, LANES)
    dtypes = (F32, F32, F32, BF16, F32, F32, BF16, BF16, BF16, F32, BF16)
    return pl.pallas_call(
        _ffn_in_kernel,
        grid=(n // tm,),
        in_specs=[row(D_MODEL), par((1, D_MODEL)), par((D_MODEL, D_FF)),
                  par((D_MODEL, D_FF)), par((D_FF, D_MODEL)), par((1, D_MODEL)),
                  par((1, D_MODEL)), par((D_MODEL, C_END)),
                  rope_spec, rope_spec, rope_spec],
        out_specs=[row(w) for w in widths],
        out_shape=[jax.ShapeDtypeStruct((n, w), d) for w, d in zip(widths, dtypes)],
        compiler_params=pltpu.CompilerParams(dimension_semantics=("arbitrary",),
                                             vmem_limit_bytes=VMEM_LIMIT),
        name="ffn_in",
    )(x, lw["n1"], lw["wg1"], lw["wu1"], lw["wd1"], lw["n1p"], lw["nm"], lw["win"], *rope)


def _ffn_out_kernel(h_ref, ycp_ref, ya_ref, wocp_ref, woa_ref, nmp_ref,
                    n2_ref, wg_ref, wu_ref, wd_ref, n2p_ref, o_ref):
    y = (jnp.dot(ycp_ref[...], wocp_ref[...], preferred_element_type=F32)
         + jnp.dot(ya_ref[...], woa_ref[...], preferred_element_type=F32))
    h = h_ref[...] + _rms(y, nmp_ref[...])
    o_ref[...] = _macaron(h, n2_ref, wg_ref, wu_ref, wd_ref, n2p_ref)


def _ffn_out(h, ycp, ya, lw, layer, tm):
    n = h.shape[0]
    row = lambda w: pl.BlockSpec((tm, w), lambda i: (i, 0))
    par = functools.partial(_layer_spec, layer)
    return pl.pallas_call(
        _ffn_out_kernel,
        grid=(n // tm,),
        in_specs=[row(D_MODEL), row(CONV_W + POOL_W), row(ATTN_W),
                  par((CONV_W + POOL_W, D_MODEL)), par((ATTN_W, D_MODEL)),
                  par((1, D_MODEL)), par((1, D_MODEL)),
                  par((D_MODEL, D_FF)), par((D_MODEL, D_FF)),
                  par((D_FF, D_MODEL)), par((1, D_MODEL))],
        out_specs=row(D_MODEL),
        out_shape=jax.ShapeDtypeStruct((n, D_MODEL), F32),
        compiler_params=pltpu.CompilerParams(dimension_semantics=("arbitrary",),
                                             vmem_limit_bytes=VMEM_LIMIT),
        name="ffn_out",
    )(h, ycp, ya, lw["wocp"], lw["woa"], lw["nmp"], lw["n2"], lw["wg2"], lw["wu2"], lw["wd2"], lw["n2p"])


CONV_HIST = 32
POOL_HIST = 16


def _mix_seq_kernel(g_ref, x_ref, pc_ref, pp_ref, dw_ref, dwb_ref, lng_ref, lnb_ref, pw_ref,
                    plw_ref, psc_ref, y_ref, nc_ref, np_ref, gbuf, pbuf, sbuf, *, tt, ttp, pos0, n_t):
    t = pl.program_id(1)
    c0, p0 = CONV_HIST - (CONV_K - 1), POOL_HIST - POOL_BUF

    @pl.when(t == 0)
    def _():
        gbuf[c0:CONV_HIST, :] = pc_ref[0]
        pbuf[p0:POOL_HIST, :] = pp_ref[0]

    if ttp != tt:
        gbuf[CONV_HIST:CONV_HIST + ttp, :] = jnp.zeros((ttp, CONV_W), F32)
        pbuf[POOL_HIST:POOL_HIST + ttp, :] = jnp.zeros((ttp, POOL_W), F32)
    gbuf[CONV_HIST:CONV_HIST + tt, :] = g_ref[0]
    pbuf[POOL_HIST:POOL_HIST + tt, :] = x_ref[0]

    acc = jnp.zeros((ttp, CONV_W), F32) + dwb_ref[...]
    for b in range(8):
        taps = range(b, CONV_K, 8)
        span = ttp + taps[-1] - b
        sbuf[0:span, :] = gbuf[c0 + b:c0 + b + span, :]
        for j in taps:
            acc = acc + sbuf[j - b:j - b + ttp, :] * dw_ref[j:j + 1, :]
    mu = jnp.mean(acc, axis=-1, keepdims=True)
    d = acc - mu
    var = jnp.mean(d * d, axis=-1, keepdims=True)
    yn = d * lax.rsqrt(var + NORM_EPS) * lng_ref[...] + lnb_ref[...]
    yn = yn * jax.nn.sigmoid(yn)
    yc = jnp.dot(yn.astype(BF16), pw_ref[...], preferred_element_type=F32)

    def shifted(j, blk):
        return pbuf[POOL_HIST - j:POOL_HIST - j + ttp, blk * LANES:(blk + 1) * LANES]

    pos = pos0 + t * tt + lax.broadcasted_iota(jnp.int32, (ttp, LANES), 0)
    lane = lax.broadcasted_iota(jnp.int32, (ttp, LANES), 1)
    lo = lane < POOL_GW
    sums = []
    for blk, (w_lo, w_hi) in enumerate(((POOL_WINDOWS[0], POOL_WINDOWS[1]),
                                        (POOL_WINDOWS[2], POOL_WINDOWS[3]))):
        s = shifted(0, blk)
        for j in range(1, w_lo):
            s = s + shifted(j, blk)
        s_lo = s
        for j in range(w_lo, w_hi):
            s = s + shifted(j, blk)
        cnt = jnp.minimum(jnp.where(lo, w_lo, w_hi), pos + 1).astype(F32)
        sums.append(jnp.where(lo, s_lo, s) / cnt - shifted(0, blk))
    pooled = jnp.concatenate(sums, axis=1)
    yp = jnp.dot(pooled.astype(BF16), plw_ref[...], preferred_element_type=F32) * psc_ref[...]

    y_ref[0, :, 0:CONV_W] = yc[0:tt].astype(BF16)
    y_ref[0, :, CONV_W:CONV_W + POOL_W] = yp[0:tt].astype(BF16)

    @pl.when(t == n_t - 1)
    def _():
        nc_ref[0] = gbuf[c0 + tt:CONV_HIST + tt, :]
        np_ref[0] = pbuf[p0 + tt:POOL_HIST + tt, :]

    if n_t > 1:
        @pl.when(t < n_t - 1)
        def _():
            gbuf[0:CONV_HIST, :] = gbuf[tt:tt + CONV_HIST, :]
            pbuf[0:POOL_HIST, :] = pbuf[tt:tt + POOL_HIST, :]


def _mix_seq(g, x, prev_conv, prev_pool, lw, layer, pos0, tt):
    b, t_len, _ = g.shape
    n_t = t_len // tt
    assert n_t * tt == t_len and (n_t == 1 or tt % 8 == 0 and tt >= CONV_HIST)
    ttp = -(-tt // 8) * 8
    tile = lambda w: pl.BlockSpec((1, tt, w), lambda i, j: (i, j, 0))
    state = lambda r, w: pl.BlockSpec((1, r, w), lambda i, j: (i, 0, 0))
    par = functools.partial(_layer_spec, layer)
    return pl.pallas_call(
        functools.partial(_mix_seq_kernel, tt=tt, ttp=ttp, pos0=pos0, n_t=n_t),
        grid=(b, n_t),
        in_specs=[tile(CONV_W), tile(POOL_W), state(CONV_K - 1, CONV_W), state(POOL_BUF, POOL_W),
                  par((CONV_K, CONV_W)), par((1, CONV_W)), par((1, CONV_W)),
                  par((1, CONV_W)), par((CONV_W, CONV_W)),
                  par((POOL_W, POOL_W)), par((1, POOL_W))],
        out_specs=[tile(CONV_W + POOL_W), state(CONV_K - 1, CONV_W), state(POOL_BUF, POOL_W)],
        out_shape=[jax.ShapeDtypeStruct((b, t_len, CONV_W + POOL_W), BF16),
                   jax.ShapeDtypeStruct((b, CONV_K - 1, CONV_W), F32),
                   jax.ShapeDtypeStruct((b, POOL_BUF, POOL_W), F32)],
        scratch_shapes=[pltpu.VMEM((CONV_HIST + ttp, CONV_W), F32),
                        pltpu.VMEM((POOL_HIST + ttp, POOL_W), F32),
                        pltpu.VMEM((CONV_HIST + ttp, CONV_W), F32)],
        compiler_params=pltpu.CompilerParams(dimension_semantics=("arbitrary", "arbitrary"),
                                             vmem_limit_bytes=VMEM_LIMIT),
        name="mix_seq",
    )(g, x, prev_conv, prev_pool, lw["dw"], lw["dwb"], lw["lng"], lw["lnb"], lw["pw"],
      lw["plw"], lw["psc"])


COUNT_GROUPS = 8


def _column_count(mask, negate=False):
    n = mask.shape[0] // 8
    mask = mask.reshape(n, 8, LANES)
    accs = [jnp.zeros((8, LANES), F32) for _ in range(min(COUNT_GROUPS, n))]
    for i in range(n):
        a = accs[i % len(accs)]
        accs[i % len(accs)] = jnp.where(mask[i], a, a + 1.0) if negate else jnp.where(mask[i], a + 1.0, a)
    while len(accs) > 1:
        accs = [a + b for a, b in zip(accs[0::2], accs[1::2])] + accs[len(accs) & ~1:]
    return jnp.sum(accs[0], axis=0, keepdims=True)


def _ordered_bits_to_float(key):
    return pltpu.bitcast(key ^ ((key >> 31) & 0x7FFFFFFF), F32)


SELECT_STEPS = 32


def _select_bias_t(score_t, valid_t, topk, sc_ref):
    w = score_t.shape[0]
    sc_ref[...] = jnp.where(valid_t, score_t, -jnp.inf)

    def bisect(it, key):
        cand = key + jnp.left_shift(jnp.int32(1), SELECT_STEPS - 1 - it)
        cnt = _column_count(sc_ref[...] < _ordered_bits_to_float(cand), negate=True)
        return jnp.where(cnt >= topk, cand, key)

    key = lax.fori_loop(0, SELECT_STEPS, bisect, jnp.full((1, LANES), INT32_MIN, jnp.int32))
    thr = _ordered_bits_to_float(key)
    sc = sc_ref[...]
    gt = sc > thr
    need = topk - _column_count(gt)
    eq = sc == thr
    eqf = jnp.where(eq, 1.0, 0.0).astype(BF16)
    r = lax.broadcasted_iota(jnp.int32, (2 * LANES, LANES), 0)
    c = lax.broadcasted_iota(jnp.int32, (2 * LANES, LANES), 1)
    tri = jnp.where((r >= LANES) | (c < r), 1.0, 0.0).astype(BF16)
    offset = jnp.zeros((1, LANES), F32)
    ranks = []
    n_ch = w // LANES
    for ch0 in range(0, n_ch, 2):
        chunks = range(ch0, min(ch0 + 2, n_ch))
        rhs = jnp.concatenate([eqf[ch * LANES:(ch + 1) * LANES, :] for ch in chunks], axis=1)
        res2 = jnp.dot(tri, rhs, preferred_element_type=F32)
        for ch in chunks:
            res = res2[:, (ch - ch0) * LANES:(ch - ch0 + 1) * LANES]
            ranks.append(res[0:LANES] + offset)
            offset = offset + res[LANES:LANES + 1]
    rank = jnp.concatenate(ranks, axis=0)
    sel = valid_t & (gt | (eq & (rank < need)))
    return jnp.where(sel, 0.0, -jnp.inf)


def _dsa_prompt_kernel(q_ref, qi_ref, wi_ref, k_ref, v_ref, ki_ref, o_ref, sc_ref, *, w, blk0, topk):
    t0 = (blk0 + pl.program_id(1)) * Q_BLOCK
    ki = ki_ref[0]
    wi_t = jnp.transpose(wi_ref[0])
    scale = IDX_DIM ** -0.5 * IDX_HEADS ** -0.5
    score_t = jnp.zeros((w, Q_BLOCK), F32)
    for h0 in range(0, IDX_HEADS, 2):
        qh = jnp.concatenate([qi_ref[0, :, h * LANES:(h + 1) * LANES] for h in (h0, h0 + 1)], axis=0)
        d2 = lax.dot_general(ki, qh, (((1,), (1,)), ((), ())), preferred_element_type=F32)
        for h in (h0, h0 + 1):
            d = d2[:, (h - h0) * Q_BLOCK:(h - h0 + 1) * Q_BLOCK]
            score_t = score_t + jnp.maximum(d, 0.0) * (wi_t[IDX_DIM + h:IDX_DIM + h + 1, :] * scale)
    s_idx = lax.broadcasted_iota(jnp.int32, (w, Q_BLOCK), 0)
    t_idx = t0 + lax.broadcasted_iota(jnp.int32, (w, Q_BLOCK), 1)
    bias = jnp.transpose(_select_bias_t(score_t, s_idx <= t_idx, topk, sc_ref))

    k = k_ref[0]
    v = v_ref[0]
    lane = lax.broadcasted_iota(jnp.int32, (Q_BLOCK, LANES), 1)
    lo = lane < HEAD_DIM
    rep = N_HEADS // N_KV_HEADS
    outs = []
    for g in range(N_KV_HEADS):
        qs = [jnp.where(lo if g == 0 else ~lo, q_ref[0, :, i * LANES:(i + 1) * LANES], 0)
              for i in range(rep)]
        qg = jnp.concatenate(qs, axis=0)
        s = lax.dot_general(qg, k, (((1,), (1,)), ((), ())), preferred_element_type=F32)
        s = s.reshape(rep, Q_BLOCK, w) + bias[None]
        m = jnp.max(s, axis=-1, keepdims=True)
        p = jnp.exp2(s - m)
        l = jnp.sum(p, axis=-1, keepdims=True)
        o = jnp.dot(p.reshape(rep * Q_BLOCK, w).astype(BF16), v, preferred_element_type=F32)
        outs.append(o.reshape(rep, Q_BLOCK, LANES) / l)
    for i in range(rep):
        o_ref[0, :, i * LANES:(i + 1) * LANES] = jnp.where(lo, outs[0][i], outs[1][i]).astype(BF16)


def _dsa_prompt(q, qi, kiwi, kb, vb, kiwib, topk):
    b, t_len, _ = q.shape
    nblk = t_len // Q_BLOCK
    first = min(nblk, max(1, -(-topk // Q_BLOCK)))
    starts = [0] + list(range(first, nblk)) + [nblk]
    outs = []
    for blk0, blk1 in zip(starts[:-1], starts[1:]):
        nb = blk1 - blk0
        w = blk1 * Q_BLOCK
        qspec = lambda wd: pl.BlockSpec((1, Q_BLOCK, wd), lambda i, j, blk0=blk0: (i, blk0 + j, 0))
        kspec = pl.BlockSpec((1, w, LANES), lambda i, j: (i, 0, 0))
        outs.append(pl.pallas_call(
            functools.partial(_dsa_prompt_kernel, w=w, blk0=blk0, topk=topk),
            grid=(b, nb),
            in_specs=[qspec(ATTN_W), qspec(IDX_HEADS * LANES), qspec(LANES), kspec, kspec, kspec],
            out_specs=pl.BlockSpec((1, Q_BLOCK, ATTN_W), lambda i, j: (i, j, 0)),
            out_shape=jax.ShapeDtypeStruct((b, nb * Q_BLOCK, ATTN_W), BF16),
            scratch_shapes=[pltpu.VMEM((w, Q_BLOCK), F32)],
            compiler_params=pltpu.CompilerParams(dimension_semantics=("arbitrary", "arbitrary"),
                                                 vmem_limit_bytes=VMEM_LIMIT),
            name=f"dsa_prompt_w{w}",
        )(q, qi, kiwi, kb, vb, kiwib))
    return jnp.concatenate(outs, axis=1) if len(outs) > 1 else outs[0]


def _fetch_pages(pt_ref, cache_ref, layer, buf, sem, seq, slot, n_pages, start):
    for j in range(n_pages):
        page = pt_ref[seq, j] if start else 0
        cp = pltpu.make_async_copy(cache_ref.at[layer, page],
                                   buf.at[slot, :, pl.ds(j * PAGE_SIZE, PAGE_SIZE)], sem.at[slot])
        if start:
            cp.start()
        else:
            cp.wait()


def _pipelined_pages(pt_ref, caches, layer, bufs, sems, n_pages):
    b, nb = pl.program_id(0), pl.num_programs(0)

    @pl.when(b == 0)
    def _():
        for cache, buf, sem in zip(caches, bufs, sems):
            _fetch_pages(pt_ref, cache, layer, buf, sem, 0, 0, n_pages, True)

    @pl.when(b + 1 < nb)
    def _():
        for cache, buf, sem in zip(caches, bufs, sems):
            _fetch_pages(pt_ref, cache, layer, buf, sem, b + 1, (b + 1) % 2, n_pages, True)

    for cache, buf, sem in zip(caches, bufs, sems):
        _fetch_pages(pt_ref, cache, layer, buf, sem, b, b % 2, n_pages, False)
    return b % 2


def _sample_index_kernel(pt_ref, qi_ref, w_ref, kinew_ref, cache_ref, o_ref, buf, sem, *, layer, n_pages):
    slot = _pipelined_pages(pt_ref, (cache_ref,), layer, (buf,), (sem,), n_pages)
    past = n_pages * PAGE_SIZE
    qi = qi_ref[0]
    wcol = w_ref[0][:, 0:1] * (IDX_DIM ** -0.5 * IDX_HEADS ** -0.5)
    kb = buf[slot].astype(BF16)
    d = jnp.dot(qi[:, 0:IDX_DIM], kb, preferred_element_type=F32)
    o_ref[0, :, 0:past] = jnp.sum(jnp.maximum(d, 0.0) * wcol, axis=0, keepdims=True)
    d_new = jnp.sum(qi.astype(F32) * kinew_ref[0].astype(F32), axis=1, keepdims=True)
    s_new = jnp.sum(jnp.maximum(d_new, 0.0) * wcol, axis=0, keepdims=True)
    lane = lax.broadcasted_iota(jnp.int32, (1, LANES), 1)
    o_ref[0, :, past:past + LANES] = jnp.where(lane == 0, s_new, -jnp.inf)


def _sample_select_kernel(s_ref, o_ref, sc_ref, *, n_valid, topk):
    bs, w = s_ref.shape
    pad = jnp.full((LANES - bs, w), -jnp.inf, F32)
    score_t = jnp.transpose(jnp.concatenate([s_ref[...], pad], axis=0))
    row = lax.broadcasted_iota(jnp.int32, (w, LANES), 0)
    bias_t = _select_bias_t(score_t, row < n_valid, topk, sc_ref)
    o_ref[...] = jnp.transpose(bias_t)[0:bs]


def _sample_attend_kernel(pt_ref, q_ref, bias_ref, knew_ref, vnew_ref, ck_ref, cv_ref, o_ref,
                          kbuf, vbuf, ksem, vsem, *, layer, n_pages):
    slot = _pipelined_pages(pt_ref, (ck_ref, cv_ref), layer, (kbuf, vbuf), (ksem, vsem), n_pages)
    past = n_pages * PAGE_SIZE
    q = q_ref[0]
    s = jnp.dot(q, kbuf[slot].astype(BF16), preferred_element_type=F32)
    s = s + bias_ref[0, :, 0:past]
    s_new = (jnp.sum(q.astype(F32) * knew_ref[0].astype(F32), axis=1, keepdims=True)
             + bias_ref[0, :, past:past + 1])
    m = jnp.maximum(jnp.max(s, axis=1, keepdims=True), s_new)
    p = jnp.exp2(s - m)
    p_new = jnp.exp2(s_new - m)
    l = jnp.sum(p, axis=1, keepdims=True) + p_new
    o = lax.dot_general(p.astype(BF16), vbuf[slot].astype(BF16), (((1,), (1,)), ((), ())),
                        preferred_element_type=F32)
    o = (o + p_new.astype(BF16).astype(F32) * vnew_ref[0].astype(F32)) / l
    lane = lax.broadcasted_iota(jnp.int32, (1, LANES), 1)
    rep = N_HEADS // N_KV_HEADS
    for i in range(rep):
        o_ref[0, :, i * LANES:(i + 1) * LANES] = jnp.where(
            lane < HEAD_DIM, o[i:i + 1], o[i + rep:i + rep + 1]).astype(BF16)


def _dsa_sample(q, qi, kiwi, kb, vb, kiwib, cache_k, cache_v, cache_kidx, page_table, layer, topk):
    bs = q.shape[0]
    n_pages = page_table.shape[1]
    past = n_pages * PAGE_SIZE
    wp = past + LANES
    rows = 16
    rep = N_HEADS // N_KV_HEADS
    lane = jnp.arange(LANES)

    qi16 = jnp.pad(qi.reshape(bs, IDX_HEADS, LANES), ((0, 0), (0, rows - IDX_HEADS), (0, 0)))
    w16 = jnp.pad(jnp.broadcast_to(kiwi[:, IDX_DIM:IDX_DIM + IDX_HEADS, None], (bs, IDX_HEADS, LANES)),
                  ((0, 0), (0, rows - IDX_HEADS), (0, 0)))
    seq = lambda r, wd: pl.BlockSpec((1, r, wd), lambda i, pt: (i, 0, 0))
    any_spec = pl.BlockSpec(memory_space=pl.ANY)
    n_phys = cache_k.shape[1]
    ckidx_t = jnp.swapaxes(cache_kidx, 2, 3)
    ck_t = cache_k.transpose(0, 1, 3, 4, 2).reshape(cache_k.shape[0], n_phys, LANES, PAGE_SIZE)
    cv_t = cache_v.transpose(0, 1, 3, 4, 2).reshape(cache_v.shape[0], n_phys, LANES, PAGE_SIZE)
    scores = pl.pallas_call(
        functools.partial(_sample_index_kernel, layer=layer, n_pages=n_pages),
        grid_spec=pltpu.PrefetchScalarGridSpec(
            num_scalar_prefetch=1, grid=(bs,),
            in_specs=[seq(rows, LANES), seq(rows, LANES), seq(1, LANES), any_spec],
            out_specs=seq(1, wp),
            scratch_shapes=[pltpu.VMEM((2, IDX_DIM, past), F32), pltpu.SemaphoreType.DMA((2,))]),
        out_shape=jax.ShapeDtypeStruct((bs, 1, wp), F32),
        compiler_params=pltpu.CompilerParams(dimension_semantics=("arbitrary",),
                                             vmem_limit_bytes=VMEM_LIMIT),
        name="sample_index",
    )(page_table, qi16, w16, kiwib.reshape(bs, 1, LANES), ckidx_t)

    bias = pl.pallas_call(
        functools.partial(_sample_select_kernel, n_valid=past + 1, topk=topk),
        out_shape=jax.ShapeDtypeStruct((bs, wp), F32),
        scratch_shapes=[pltpu.VMEM((wp, LANES), F32)],
        compiler_params=pltpu.CompilerParams(vmem_limit_bytes=VMEM_LIMIT),
        name="sample_select",
    )(scores.reshape(bs, wp)).reshape(bs, 1, wp)

    q4 = q.reshape(bs, rep, LANES)
    q16 = jnp.concatenate([jnp.where(lane < HEAD_DIM, q4, 0), jnp.where(lane >= HEAD_DIM, q4, 0),
                           jnp.zeros((bs, rows - N_HEADS, LANES), q.dtype)], axis=1)
    return pl.pallas_call(
        functools.partial(_sample_attend_kernel, layer=layer, n_pages=n_pages),
        grid_spec=pltpu.PrefetchScalarGridSpec(
            num_scalar_prefetch=1, grid=(bs,),
            in_specs=[seq(rows, LANES), seq(1, wp), seq(1, LANES), seq(1, LANES), any_spec, any_spec],
            out_specs=seq(1, ATTN_W),
            scratch_shapes=[pltpu.VMEM((2, LANES, past), F32), pltpu.VMEM((2, LANES, past), F32),
                            pltpu.SemaphoreType.DMA((2,)), pltpu.SemaphoreType.DMA((2,))]),
        out_shape=jax.ShapeDtypeStruct((bs, 1, ATTN_W), BF16),
        compiler_params=pltpu.CompilerParams(dimension_semantics=("arbitrary",),
                                             vmem_limit_bytes=VMEM_LIMIT),
        name="sample_attend",
    )(page_table, q16, bias, kb.reshape(bs, 1, LANES), vb.reshape(bs, 1, LANES), ck_t, cv_t).reshape(bs, ATTN_W)


def _head_pairs(x, axis):
    rep = N_HEADS // N_KV_HEADS
    shape = x.shape
    x = x.reshape(shape[:axis] + (N_KV_HEADS, rep, HEAD_DIM) + shape[axis + 1:])
    return jnp.swapaxes(x, axis, axis + 1).reshape(shape)


def _prep_weights(w_in, w_out, pool_w):
    depth = w_in.shape[0]
    w_in, w_out = w_in.astype(BF16), w_out.astype(BF16)
    o_q, o_k = 2 * CONV_W + POOL_W, 2 * CONV_W + POOL_W + ATTN_W
    o_qi = o_k + 2 * N_KV_HEADS * HEAD_DIM
    o_ki = o_qi + IDX_HEADS * IDX_DIM
    zeros = lambda n: jnp.zeros((depth, D_MODEL, n), w_in.dtype)
    parts = [w_in[..., :o_q], _head_pairs(w_in[..., o_q:o_k], 2), w_in[..., o_k:o_qi]]
    for h in range(IDX_HEADS):
        parts += [w_in[..., o_qi + h * IDX_DIM:o_qi + (h + 1) * IDX_DIM], zeros(LANES - IDX_DIM)]
    parts += [w_in[..., o_ki:], zeros(LANES - IDX_DIM - IDX_HEADS)]
    win = jnp.concatenate(parts, axis=-1)
    wocp = w_out[:, :CONV_W + POOL_W]
    woa = _head_pairs(w_out[:, CONV_W + POOL_W:], 1)
    plw = jnp.zeros((depth, POOL_W, POOL_W), pool_w.dtype)
    for g in range(len(POOL_WINDOWS)):
        plw = plw.at[:, g * POOL_GW:(g + 1) * POOL_GW, g * POOL_GW:(g + 1) * POOL_GW].set(pool_w[:, g])
    return win, wocp, woa, plw.astype(BF16)


def kernel(x_prompt, x_sample, cache_k, cache_v, cache_kidx, state_conv, state_pool, page_table,
           ffn1_norm_pre, ffn1_w_gate, ffn1_w_up, ffn1_w_down, ffn1_norm_post,
           mix_norm_pre, w_in, conv_dw, conv_dw_bias, conv_ln_gain, conv_ln_bias, conv_pw,
           pool_w, pool_scale, w_out, mix_norm_post,
           ffn2_norm_pre, ffn2_w_gate, ffn2_w_up, ffn2_w_down, ffn2_norm_post):
    bp, tp, _ = x_prompt.shape
    bs, ts, _ = x_sample.shape
    assert ts == 1 and tp % Q_BLOCK == 0
    depth = w_in.shape[0]
    past = page_table.shape[1] * PAGE_SIZE

    win, wocp, woa, plw = _prep_weights(w_in, w_out, pool_w)
    vec = lambda a: a.reshape(depth, 1, -1)
    stacks = dict(
        n1=vec(ffn1_norm_pre), wg1=ffn1_w_gate.astype(BF16), wu1=ffn1_w_up.astype(BF16),
        wd1=ffn1_w_down.astype(BF16), n1p=vec(ffn1_norm_post), nm=vec(mix_norm_pre), win=win,
        dw=conv_dw, dwb=vec(conv_dw_bias), lng=vec(conv_ln_gain), lnb=vec(conv_ln_bias),
        pw=conv_pw.astype(BF16), plw=plw, psc=vec(pool_scale), wocp=wocp, woa=woa,
        nmp=vec(mix_norm_post), n2=vec(ffn2_norm_pre), wg2=ffn2_w_gate.astype(BF16),
        wu2=ffn2_w_up.astype(BF16), wd2=ffn2_w_down.astype(BF16), n2p=vec(ffn2_norm_post))

    tm_p = 512 if tp % 512 == 0 else Q_BLOCK
    tt_p = 512 if tp % 512 == 0 else Q_BLOCK
    rope_p = _rope_tables(tp, 0, False)
    rope_s = _rope_tables(bs * ts, past, True)
    topk_p = min(INDEX_TOPK, tp // 4)
    topk_s = min(INDEX_TOPK, (past + ts) // 4)
    zc = jnp.zeros((bp, CONV_K - 1, CONV_W), F32)
    zp = jnp.zeros((bp, POOL_BUF, POOL_W), F32)

    hp = x_prompt.reshape(bp * tp, D_MODEL)
    hs = x_sample.reshape(bs * ts, D_MODEL)
    acc_p, acc_s = [], []
    lw = stacks
    for l in range(depth):
        hp, g, xpool, q, k, v, kb, vb, qi, kiwi, kiwib = _ffn_in(hp, lw, l, rope_p, tm_p, tp // tm_p)
        seq = lambda a: a.reshape(bp, tp, a.shape[-1])
        ycp, nconv, npool = _mix_seq(seq(g), seq(xpool), zc, zp, lw, l, 0, tt_p)
        ya = _dsa_prompt(seq(q), seq(qi), seq(kiwi), seq(kb), seq(vb), seq(kiwib), topk_p)
        hp = _ffn_out(hp, ycp.reshape(bp * tp, -1), ya.reshape(bp * tp, -1), lw, l, tm_p)
        acc_p.append((k.reshape(bp, tp, N_KV_HEADS, HEAD_DIM), v.reshape(bp, tp, N_KV_HEADS, HEAD_DIM),
                      kiwi[:, :IDX_DIM].reshape(bp, tp, IDX_DIM), nconv, npool))
        hs, g, xpool, q, k, v, kb, vb, qi, kiwi, kiwib = _ffn_in(hs, lw, l, rope_s, bs * ts, 1)
        seq = lambda a: a.reshape(bs, ts, a.shape[-1])
        ycp, nconv, npool = _mix_seq(seq(g), seq(xpool), state_conv[l], state_pool[l], lw, l, past, ts)
        ya = _dsa_sample(q, qi, kiwi, kb, vb, kiwib, cache_k, cache_v, cache_kidx, page_table, l, topk_s)
        hs = _ffn_out(hs, ycp.reshape(bs * ts, -1), ya, lw, l, bs * ts)
        acc_s.append((k.reshape(bs, ts, N_KV_HEADS, HEAD_DIM), v.reshape(bs, ts, N_KV_HEADS, HEAD_DIM),
                      kiwi[:, :IDX_DIM].reshape(bs, ts, IDX_DIM), nconv, npool))

    stack = lambda acc, i: jnp.stack([a[i] for a in acc])
    return ((hp.reshape(bp, tp, D_MODEL), hs.reshape(bs, ts, D_MODEL))
            + tuple(stack(acc_p, i) for i in range(5)) + tuple(stack(acc_s, i) for i in range(5)))
```
